```python
import math
import jax, jax.numpy as jnp
from jax import lax
import numpy as np

D_MODEL = 2048
BATCH = 2
SEQ = 4096
DEPTH = 2
DEC_BATCH = 128
DEC_SEQ = 8
PAST_LEN = 16384
PAGE_SIZE = 128

N_EVEN = (DEPTH + 1) // 2
N_ODD = DEPTH // 2
S5_WIDTH = D_MODEL // 2
S5_GROUP = 16
S5_GROUPS = S5_WIDTH // S5_GROUP
S5_STATE = 64
S5_DT_MIN = 1e-3
S5_DT_MAX = 1e-1
MLA_HEADS = 8
D_NOPE = 128
D_ROPE = 64
D_V = 128
D_LATENT = 512
ROPE_THETA = 10000.0
MLA_SCALE = (D_NOPE + D_ROPE) ** -0.5
Q_BLOCK = 128
GLA_HEADS = 4
GLA_DK = 128
GLA_DV = 256
GLA_GATE_RANK = 16
GLA_GATE_NORM = 16.0
ML_HEADS = 4
ML_DK = 128
ML_DV = 256
ML_F_BIAS = 3.0
CHUNK = 64
D_FF = -(-8 * D_MODEL // (3 * 256)) * 256
EPS = 1e-6

E_SIZES = (S5_WIDTH, MLA_HEADS * (D_NOPE + D_ROPE), D_LATENT, D_ROPE)
O_SIZES = (GLA_HEADS * GLA_DK, GLA_HEADS * GLA_DK, GLA_HEADS * GLA_DV, GLA_GATE_RANK, GLA_HEADS * GLA_DV,
           ML_HEADS * ML_DK, ML_HEADS * ML_DK, ML_HEADS * ML_DV, ML_HEADS, ML_HEADS, ML_HEADS * ML_DV)
E_IN = sum(E_SIZES)
O_IN = sum(O_SIZES)
E_MIX = S5_WIDTH + MLA_HEADS * D_V
O_MIX = GLA_HEADS * GLA_DV + ML_HEADS * ML_DV

kernel_name = "hybrid_s5_mla_gla_mlstm_step"


def split_cols(t, sizes):
    return jnp.split(t, [int(s) for s in np.cumsum(sizes)[:-1]], axis=-1)


def rmsnorm(x, g):
    xf = x.astype(jnp.float32)
    y = xf * lax.rsqrt(jnp.mean(xf * xf, axis=-1, keepdims=True) + EPS)
    return (y * g.astype(jnp.float32)).astype(x.dtype)


def rope(x, pos):
    half = D_ROPE // 2
    inv_freq = ROPE_THETA ** (-jnp.arange(half, dtype=jnp.float32) / half)
    ang = pos.astype(jnp.float32)[:, None] * inv_freq
    ang = ang.reshape((ang.shape[0],) + (1,) * (x.ndim - 3) + (half,))
    cos, sin = jnp.cos(ang), jnp.sin(ang)
    xf = x.astype(jnp.float32)
    x1, x2 = xf[..., :half], xf[..., half:]
    return jnp.concatenate([x1 * cos - x2 * sin, x1 * sin + x2 * cos], axis=-1).astype(x.dtype)


def swiglu(x, w1, w3, w2):
    return (jax.nn.silu(x @ w1) * (x @ w3)) @ w2


def s5_mixer(u, h0_re, h0_im, a_re, a_im, log_dt, b_re, b_im, c_re, c_im, d_skip, w_glu, b_glu):
    bsz, length, _ = u.shape
    f32 = jnp.float32
    lam = lax.complex(jnp.minimum(a_re.astype(f32), -1e-4), a_im.astype(f32))
    lam_bar = jnp.exp(lam * jnp.exp(log_dt.astype(f32))[:, None])
    b_bar = ((lam_bar - 1.0) / lam)[..., None] * lax.complex(b_re.astype(f32), b_im.astype(f32))
    uf = u.astype(f32)
    ug = uf.reshape(bsz, length, S5_GROUPS, S5_GROUP).astype(jnp.complex64)
    bu = jnp.einsum("blgi,gpi->blgp", ug, b_bar)
    bu = bu.at[:, 0].add(lam_bar * lax.complex(h0_re.astype(f32), h0_im.astype(f32)))

    def combine(e1, e2):
        a1, s1 = e1
        a2, s2 = e2
        return a2 * a1, a2 * s1 + s2

    _, h = lax.associative_scan(combine, (jnp.broadcast_to(lam_bar, bu.shape), bu), axis=1)
    cc = lax.complex(c_re.astype(f32), c_im.astype(f32))
    y = jnp.einsum("blgp,gip->blgi", h, cc).real.reshape(bsz, length, S5_WIDTH) + d_skip.astype(f32) * uf
    z = jax.nn.gelu(y)
    out = z * jax.nn.sigmoid(z @ w_glu.astype(f32) + b_glu.astype(f32))
    h_last = h[:, -1]
    return out.astype(u.dtype), h_last.real, h_last.imag


def mla_softmax_attend(q_lat, q_rope, q_pos, k_lat, k_rope, k_rs, k_pos):
    f32 = jnp.float32
    s = jnp.einsum("bqhc,bkc->bhqk", q_lat, k_lat, preferred_element_type=f32)
    s = s * jnp.swapaxes(k_rs, 1, 2)[:, :, None, :].astype(f32)
    s = s + jnp.einsum("bqhr,bkr->bhqk", q_rope, k_rope, preferred_element_type=f32)
    s = jnp.where(k_pos[None, None, None, :] <= q_pos[None, None, :, None], s * MLA_SCALE, -jnp.inf)
    p = jax.nn.softmax(s, axis=-1)
    return jnp.einsum("bhqk,bkc->bqhc", p.astype(k_lat.dtype), k_lat)


def mla_attend_prompt(q_lat, q_rope, c_lat, k_rope, k_rs, pos):
    bsz, length = q_lat.shape[:2]
    blk = math.gcd(length, Q_BLOCK)
    nb = length // blk

    def blocks(t):
        return jnp.moveaxis(t.reshape((bsz, nb, blk) + t.shape[2:]), 1, 0)

    def one_block(a):
        ql, qr, qp = a
        return mla_softmax_attend(ql, qr, qp, c_lat, k_rope, k_rs, pos)

    out = lax.map(one_block, (blocks(q_lat), blocks(q_rope), pos.reshape(nb, blk)))
    return jnp.moveaxis(out, 0, 1).reshape(q_lat.shape)


def mla_attend_sample(q_lat, q_rope, c_lat, k_rope, k_rs, pos, layer, cache_lat, cache_rope, cache_rs, page_table):
    n_past = page_table.shape[1] * PAGE_SIZE
    k_pos = jnp.concatenate([jnp.arange(n_past, dtype=pos.dtype), pos])

    def one_seq(a):
        pt, ql, qr, cl, kr, rs = a
        lat = jnp.concatenate([cache_lat[layer, pt].reshape(n_past, D_LATENT).astype(cl.dtype), cl], axis=0)
        rot = jnp.concatenate([cache_rope[layer, pt].reshape(n_past, D_ROPE).astype(kr.dtype), kr], axis=0)
        rsc = jnp.concatenate([cache_rs[layer, pt].reshape(n_past, MLA_HEADS).astype(rs.dtype), rs], axis=0)
        return mla_softmax_attend(ql[None], qr[None], pos, lat[None], rot[None], rsc[None], k_pos)[0]

    return lax.map(one_seq, (page_table, q_lat, q_rope, c_lat, k_rope, k_rs))


def even_mix(xn, pos, h0_re, h0_im, attend, w_in, a_re, a_im, log_dt, b_re, b_im, c_re, c_im, d_skip,
             w_glu, b_glu, g_qn, g_qr, g_kn, g_kr, g_lat, w_uk, w_uv, w_out):
    bsz, length, _ = xn.shape
    u, q, c_lat, k_r = split_cols(xn @ w_in, E_SIZES)
    s5_y, s5_re, s5_im = s5_mixer(u, h0_re, h0_im, a_re, a_im, log_dt, b_re, b_im, c_re, c_im, d_skip, w_glu, b_glu)
    q = q.reshape(bsz, length, MLA_HEADS, D_NOPE + D_ROPE)
    q_nope = rmsnorm(q[..., :D_NOPE], g_qn)
    q_rope = rope(rmsnorm(q[..., D_NOPE:], g_qr), pos)
    c_lat = rmsnorm(c_lat, g_lat)
    k_rope = rope(rmsnorm(k_r, g_kr), pos)
    kf = jnp.einsum("blc,chn->blhn", c_lat, w_uk).astype(jnp.float32)
    k_rs = lax.rsqrt(jnp.mean(kf * kf, axis=-1) + EPS).astype(xn.dtype)
    q_lat = jnp.einsum("blhn,chn->blhc", q_nope * g_kn, w_uk)
    o_lat = attend(q_lat, q_rope, c_lat, k_rope, k_rs)
    o = jnp.einsum("blhc,chv->blhv", o_lat, w_uv).reshape(bsz, length, MLA_HEADS * D_V)
    y = jnp.concatenate([s5_y, o], axis=-1) @ w_out
    return y, (c_lat, k_rope, k_rs, s5_re, s5_im)


def gla_chunked(q, k, v, log_a, s0):
    bsz, length = q.shape[:2]
    c = math.gcd(length, CHUNK)
    n = length // c
    tri = jnp.tril(jnp.ones((c, c), dtype=bool))

    def chunks(t):
        t = t.astype(jnp.float32)
        return jnp.moveaxis(t.reshape((bsz, n, c) + t.shape[2:]), 1, 0)

    def step(S, xs):
        qc, kc, vc, gc = xs
        b = jnp.cumsum(gc, axis=1)
        o_inter = jnp.einsum("bthk,bhkv->bthv", qc * jnp.exp(b), S)
        rel = jnp.where(tri[None, :, :, None, None], b[:, :, None] - b[:, None, :], -jnp.inf)
        att = jnp.einsum("bthk,bshk,btshk->bhts", qc, kc, jnp.exp(rel))
        o_intra = jnp.einsum("bhts,bshv->bthv", att, vc)
        b_last = b[:, -1]
        S = jnp.exp(b_last)[..., None] * S + jnp.einsum("bshk,bshv->bhkv", kc * jnp.exp(b_last[:, None] - b), vc)
        return S, o_inter + o_intra

    S, o = lax.scan(step, s0.astype(jnp.float32), (chunks(q), chunks(k), chunks(v), chunks(log_a)))
    o = jnp.moveaxis(o, 0, 1).reshape((bsz, length) + o.shape[3:])
    return o.astype(q.dtype), S


def mlstm_chunked(q, k, v, i_pre, f_pre, c0, n0, m0):
    bsz, length = q.shape[:2]
    c = math.gcd(length, CHUNK)
    n = length // c
    tri = jnp.tril(jnp.ones((c, c), dtype=bool))

    def chunks(t):
        t = t.astype(jnp.float32)
        return jnp.moveaxis(t.reshape((bsz, n, c) + t.shape[2:]), 1, 0)

    def step(carry, xs):
        cm, nv, m = carry
        qc, kc, vc, ic, fc = xs
        b = jnp.cumsum(jax.nn.log_sigmoid(fc), axis=1)
        rel = jnp.where(tri[None, :, :, None], b[:, :, None] - b[:, None] + ic[:, None], -jnp.inf)
        carry_log = b + m[:, None]
        m_t = jnp.maximum(carry_log, jnp.max(rel, axis=2))
        w_intra = jnp.exp(rel - m_t[:, :, None])
        w_carry = jnp.exp(carry_log - m_t)
        qk = jnp.einsum("bthk,bshk->btsh", qc, kc) * w_intra
        num = jnp.einsum("btsh,bshv->bthv", qk, vc) + w_carry[..., None] * jnp.einsum("bthk,bhkv->bthv", qc, cm)
        den = jnp.sum(qk, axis=2) + w_carry * jnp.einsum("bthk,bhk->bth", qc, nv)
        h = num / jnp.maximum(jnp.abs(den), jnp.exp(-m_t))[..., None]
        m_new = m_t[:, -1]
        w_state = jnp.exp(b[:, -1:] - b + ic - m_new[:, None])
        decay = jnp.exp(b[:, -1] + m - m_new)
        cm = decay[..., None, None] * cm + jnp.einsum("bsh,bshk,bshv->bhkv", w_state, kc, vc)
        nv = decay[..., None] * nv + jnp.einsum("bsh,bshk->bhk", w_state, kc)
        return (cm, nv, m_new), h

    init = (c0.astype(jnp.float32), n0.astype(jnp.float32), m0.astype(jnp.float32))
    (cm, nv, m), h = lax.scan(step, init, (chunks(q), chunks(k), chunks(v), chunks(i_pre), chunks(f_pre)))
    h = jnp.moveaxis(h, 0, 1).reshape((bsz, length) + h.shape[3:])
    return h.astype(q.dtype), cm, nv, m


def odd_mix(xn, s0, c0, n0, m0, w_in, w_a2, b_a, g_gla, b_i, b_f, g_ml, w_out):
    bsz, length, _ = xn.shape
    gq, gk, gv, ga, gg, mq, mk, mv, mi, mf, mo = split_cols(xn @ w_in, O_SIZES)

    def heads(t, h):
        return t.reshape(bsz, length, h, -1)

    log_a = jax.nn.log_sigmoid((ga @ w_a2 + b_a).astype(jnp.float32)) / GLA_GATE_NORM
    o_gla, s_gla = gla_chunked(heads(gq, GLA_HEADS) * GLA_DK ** -0.5, heads(gk, GLA_HEADS), heads(gv, GLA_HEADS),
                               heads(log_a, GLA_HEADS), s0)
    o_gla = rmsnorm(o_gla, g_gla) * jax.nn.silu(heads(gg, GLA_HEADS))
    h_ml, cm, nv, m = mlstm_chunked(heads(mq, ML_HEADS) * ML_DK ** -0.5, heads(mk, ML_HEADS), heads(mv, ML_HEADS),
                                    mi + b_i, mf + b_f, c0, n0, m0)
    h_ml = jax.nn.sigmoid(heads(mo, ML_HEADS)) * rmsnorm(h_ml, g_ml)
    y = jnp.concatenate([o_gla.reshape(bsz, length, -1), h_ml.reshape(bsz, length, -1)], axis=-1) @ w_out
    return y, (s_gla, cm, nv, m)


def setup_inputs(seed: int = 0) -> dict:
    key = jax.random.key(seed)
    ks = iter(list(jax.random.split(key, 64)))

    def nrm(shape, scale):
        return scale * jax.random.normal(next(ks), shape, jnp.float32)

    def gain(shape):
        return 1.0 + nrm(shape, 0.1)

    n_pages = PAST_LEN // PAGE_SIZE
    n_used = DEC_BATCH * n_pages
    n_pool = n_used + n_used // 4
    page_table = jax.random.permutation(next(ks), n_pool)[:n_used].reshape(DEC_BATCH, n_pages).astype(jnp.int32)
    s5_a_im = jnp.pi * jnp.arange(S5_STATE, dtype=jnp.float32) + nrm((N_EVEN, S5_GROUPS, S5_STATE), 0.02)
    s5_log_dt = jax.random.uniform(next(ks), (N_EVEN, S5_GROUPS), jnp.float32,
                                   math.log(S5_DT_MIN), math.log(S5_DT_MAX))
    return {
        "x_prompt": nrm((BATCH, SEQ, D_MODEL), 1.0),
        "x_sample": nrm((DEC_BATCH, DEC_SEQ, D_MODEL), 1.0),
        "cache_mla_latent": nrm((N_EVEN, n_pool, PAGE_SIZE, D_LATENT), 1.0),
        "cache_mla_k_rope": nrm((N_EVEN, n_pool, PAGE_SIZE, D_ROPE), 1.0),
        "cache_mla_k_rscale": jnp.exp(nrm((N_EVEN, n_pool, PAGE_SIZE, MLA_HEADS), 0.1)),
        "state_s5_re": nrm((N_EVEN, DEC_BATCH, S5_GROUPS, S5_STATE), 0.3),
        "state_s5_im": nrm((N_EVEN, DEC_BATCH, S5_GROUPS, S5_STATE), 0.3),
        "state_gla": nrm((N_ODD, DEC_BATCH, GLA_HEADS, GLA_DK, GLA_DV), 0.1),
        "state_mlstm_c": nrm((N_ODD, DEC_BATCH, ML_HEADS, ML_DK, ML_DV), 0.1),
        "state_mlstm_n": nrm((N_ODD, DEC_BATCH, ML_HEADS, ML_DK), 0.1),
        "state_mlstm_m": nrm((N_ODD, DEC_BATCH, ML_HEADS), 1.0),
        "page_table": page_table,
        "norm_mix": gain((DEPTH, D_MODEL)),
        "norm_ffn": gain((DEPTH, D_MODEL)),
        "ffn_w1": nrm((DEPTH, D_MODEL, D_FF), D_MODEL ** -0.5),
        "ffn_w3": nrm((DEPTH, D_MODEL, D_FF), D_MODEL ** -0.5),
        "ffn_w2": nrm((DEPTH, D_FF, D_MODEL), D_FF ** -0.5),
        "e_w_in": nrm((N_EVEN, D_MODEL, E_IN), D_MODEL ** -0.5),
        "s5_a_re": -0.5 + nrm((N_EVEN, S5_GROUPS, S5_STATE), 0.02),
        "s5_a_im": s5_a_im,
        "s5_log_dt": s5_log_dt,
        "s5_b_re": nrm((N_EVEN, S5_GROUPS, S5_STATE, S5_GROUP), (2 * S5_GROUP) ** -0.5),
        "s5_b_im": nrm((N_EVEN, S5_GROUPS, S5_STATE, S5_GROUP), (2 * S5_GROUP) ** -0.5),
        "s5_c_re": nrm((N_EVEN, S5_GROUPS, S5_GROUP, S5_STATE), S5_STATE ** -0.5),
        "s5_c_im": nrm((N_EVEN, S5_GROUPS, S5_GROUP, S5_STATE), S5_STATE ** -0.5),
        "s5_d": nrm((N_EVEN, S5_WIDTH), 1.0),
        "s5_w_glu": nrm((N_EVEN, S5_WIDTH, S5_WIDTH), S5_WIDTH ** -0.5),
        "s5_b_glu": nrm((N_EVEN, S5_WIDTH), 0.02),
        "mla_g_qnope": gain((N_EVEN, D_NOPE)),
        "mla_g_qrope": gain((N_EVEN, D_ROPE)),
        "mla_g_knope": gain((N_EVEN, D_NOPE)),
        "mla_g_krope": gain((N_EVEN, D_ROPE)),
        "mla_g_latent": gain((N_EVEN, D_LATENT)),
        "mla_w_uk": nrm((N_EVEN, D_LATENT, MLA_HEADS, D_NOPE), D_LATENT ** -0.5),
        "mla_w_uv": nrm((N_EVEN, D_LATENT, MLA_HEADS, D_V), D_LATENT ** -0.5),
        "e_w_out": nrm((N_EVEN, E_MIX, D_MODEL), E_MIX ** -0.5),
        "o_w_in": nrm((N_ODD, D_MODEL, O_IN), D_MODEL ** -0.5),
        "gla_w_a2": nrm((N_ODD, GLA_GATE_RANK, GLA_HEADS * GLA_DK), GLA_GATE_RANK ** -0.5),
        "gla_b_a": nrm((N_ODD, GLA_HEADS * GLA_DK), 0.1),
        "gla_g_norm": gain((N_ODD, GLA_DV)),
        "ml_b_i": nrm((N_ODD, ML_HEADS), 0.1),
        "ml_b_f": ML_F_BIAS + nrm((N_ODD, ML_HEADS), 0.1),
        "ml_g_norm": gain((N_ODD, ML_DV)),
        "o_w_out": nrm((N_ODD, O_MIX, D_MODEL), O_MIX ** -0.5),
    }


def reference(x_prompt, x_sample, cache_mla_latent, cache_mla_k_rope, cache_mla_k_rscale, state_s5_re, state_s5_im,
              state_gla, state_mlstm_c, state_mlstm_n, state_mlstm_m, page_table,
              norm_mix, norm_ffn, ffn_w1, ffn_w3, ffn_w2,
              e_w_in, s5_a_re, s5_a_im, s5_log_dt, s5_b_re, s5_b_im, s5_c_re, s5_c_im, s5_d, s5_w_glu, s5_b_glu,
              mla_g_qnope, mla_g_qrope, mla_g_knope, mla_g_krope, mla_g_latent, mla_w_uk, mla_w_uv, e_w_out,
              o_w_in, gla_w_a2, gla_b_a, gla_g_norm, ml_b_i, ml_b_f, ml_g_norm, o_w_out):
    bp, lp, _ = x_prompt.shape
    bs, ls, _ = x_sample.shape
    pos_p = jnp.arange(lp, dtype=jnp.int32)
    pos_s = PAST_LEN + jnp.arange(ls, dtype=jnp.int32)
    xp, xs = x_prompt, x_sample
    even_p, even_s, odd_p, odd_s = [], [], [], []
    for l in range(DEPTH):
        j = l // 2
        hp = rmsnorm(xp, norm_mix[l])
        hs = rmsnorm(xs, norm_mix[l])
        if l % 2 == 0:
            w = (e_w_in[j], s5_a_re[j], s5_a_im[j], s5_log_dt[j], s5_b_re[j], s5_b_im[j], s5_c_re[j], s5_c_im[j],
                 s5_d[j], s5_w_glu[j], s5_b_glu[j], mla_g_qnope[j], mla_g_qrope[j], mla_g_knope[j], mla_g_krope[j],
                 mla_g_latent[j], mla_w_uk[j], mla_w_uv[j], e_w_out[j])
            zeros = jnp.zeros((bp, S5_GROUPS, S5_STATE), jnp.float32)
            attend_p = lambda ql, qr, cl, kr, rs: mla_attend_prompt(ql, qr, cl, kr, rs, pos_p)
            attend_s = lambda ql, qr, cl, kr, rs: mla_attend_sample(ql, qr, cl, kr, rs, pos_s, j, cache_mla_latent,
                                                                    cache_mla_k_rope, cache_mla_k_rscale, page_table)
            yp, st_p = even_mix(hp, pos_p, zeros, zeros, attend_p, *w)
            ys, st_s = even_mix(hs, pos_s, state_s5_re[j], state_s5_im[j], attend_s, *w)
            even_p.append(st_p)
            even_s.append(st_s)
        else:
            w = (o_w_in[j], gla_w_a2[j], gla_b_a[j], gla_g_norm[j], ml_b_i[j], ml_b_f[j], ml_g_norm[j], o_w_out[j])
            s0 = jnp.zeros((bp, GLA_HEADS, GLA_DK, GLA_DV), jnp.float32)
            c0 = jnp.zeros((bp, ML_HEADS, ML_DK, ML_DV), jnp.float32)
            n0 = jnp.zeros((bp, ML_HEADS, ML_DK), jnp.float32)
            m0 = jnp.zeros((bp, ML_HEADS), jnp.float32)
            yp, st_p = odd_mix(hp, s0, c0, n0, m0, *w)
            ys, st_s = odd_mix(hs, state_gla[j], state_mlstm_c[j], state_mlstm_n[j], state_mlstm_m[j], *w)
            odd_p.append(st_p)
            odd_s.append(st_s)
        xp = xp + yp
        xs = xs + ys
        xp = xp + swiglu(rmsnorm(xp, norm_ffn[l]), ffn_w1[l], ffn_w3[l], ffn_w2[l])
        xs = xs + swiglu(rmsnorm(xs, norm_ffn[l]), ffn_w1[l], ffn_w3[l], ffn_w2[l])
    p_lat, p_rope, p_rscale, p_s5_re, p_s5_im = [jnp.stack(t) for t in zip(*even_p)]
    s_lat, s_rope, s_rscale, s_s5_re, s_s5_im = [jnp.stack(t) for t in zip(*even_s)]
    p_gla, p_ml_c, p_ml_n, p_ml_m = [jnp.stack(t) for t in zip(*odd_p)]
    s_gla, s_ml_c, s_ml_n, s_ml_m = [jnp.stack(t) for t in zip(*odd_s)]
    return (xp, xs,
            p_lat, p_rope, p_rscale, p_s5_re, p_s5_im, p_gla, p_ml_c, p_ml_n, p_ml_m,
            s_lat, s_rope, s_rscale, s_s5_re, s_s5_im, s_gla, s_ml_c, s_ml_n, s_ml_m)
```

```python
import functools
import math

import jax
import jax.numpy as jnp
import numpy as np
from jax import lax
from jax.experimental import pallas as pl
from jax.experimental.pallas import tpu as pltpu

F32, BF16 = jnp.float32, jnp.bfloat16
SDS = jax.ShapeDtypeStruct

EPS = 1e-6
LANES = 128
SUBLANES = 8
VMEM_LIMIT = 52 * 1024 * 1024

S5_GROUP = 16
S5_STATE = 64
MLA_HEADS = 8
D_NOPE = 128
D_ROPE = 64
D_V = 128
D_LATENT = 512
ROPE_THETA = 10000.0
MLA_SCALE = (D_NOPE + D_ROPE) ** -0.5
PAGE = 128
GLA_HEADS = 4
GLA_DK = 128
GLA_DV = 256
GLA_RANK = 16
GLA_GATE_NORM = 16.0
ML_HEADS = 4
ML_DK = 128
ML_DV = 256
CHUNK = 64
NEG = -1e30


def _cparams(sem):
    return pltpu.CompilerParams(dimension_semantics=sem, vmem_limit_bytes=VMEM_LIMIT)


def _rms(x, g):
    ms = jnp.mean(x * x, axis=-1, keepdims=True)
    return x * lax.rsqrt(ms + EPS) * g


def _dot(a, b):
    return jnp.dot(a, b, preferred_element_type=F32)


def _dot_nt(a, b):
    return lax.dot_general(a, b, (((1,), (1,)), ((), ())), preferred_element_type=F32)


def _dot_tn(a, b):
    return lax.dot_general(a, b, (((0,), (0,)), ((), ())), preferred_element_type=F32)


def _split_bf16(x):
    hi = x.astype(BF16)
    lo = (x - hi.astype(F32)).astype(BF16)
    return hi, lo


def _log_sigmoid(x):
    return jnp.minimum(x, 0.0) - jnp.log(1.0 + jnp.exp(-jnp.abs(x)))


def _norm_matmul_body(x_ref, g_ref, w_ref, o_ref, xn_ref):
    @pl.when(pl.program_id(1) == 0)
    def _():
        xn_ref[...] = _rms(x_ref[...], g_ref[...]).astype(BF16)

    o_ref[...] = _dot(xn_ref[...], w_ref[...])


def norm_matmul(x, g, w, tm, tn):
    m, k = x.shape
    n = w.shape[1]
    return pl.pallas_call(
        _norm_matmul_body,
        grid=(m // tm, n // tn),
        in_specs=[
            pl.BlockSpec((tm, k), lambda i, j: (i, 0)),
            pl.BlockSpec((1, k), lambda i, j: (0, 0)),
            pl.BlockSpec((k, tn), lambda i, j: (0, j)),
        ],
        out_specs=pl.BlockSpec((tm, tn), lambda i, j: (i, j)),
        out_shape=SDS((m, n), F32),
        scratch_shapes=[pltpu.VMEM((tm, k), BF16)],
        compiler_params=_cparams(("parallel", "arbitrary")),
        name="norm_matmul",
    )(x, g, w)


def _ffn_body(x_ref, g_ref, w1_ref, w3_ref, w2_ref, o_ref, xn_ref):
    @pl.when(pl.program_id(1) == 0)
    def _():
        x = x_ref[...]
        xn_ref[...] = _rms(x, g_ref[...]).astype(BF16)
        o_ref[...] = x

    xn = xn_ref[...]
    a = _dot(xn, w1_ref[...])
    b = _dot(xn, w3_ref[...])
    h = (a * jax.nn.sigmoid(a) * b).astype(BF16)
    o_ref[...] += _dot(h, w2_ref[...])


def ffn(x, g, w1, w3, w2, tm, tf):
    m, d = x.shape
    f = w1.shape[1]
    return pl.pallas_call(
        _ffn_body,
        grid=(m // tm, f // tf),
        in_specs=[
            pl.BlockSpec((tm, d), lambda i, j: (i, 0), pipeline_mode=pl.Buffered(1)),
            pl.BlockSpec((1, d), lambda i, j: (0, 0)),
            pl.BlockSpec((d, tf), lambda i, j: (0, j)),
            pl.BlockSpec((d, tf), lambda i, j: (0, j)),
            pl.BlockSpec((tf, d), lambda i, j: (j, 0)),
        ],
        out_specs=pl.BlockSpec((tm, d), lambda i, j: (i, 0)),
        out_shape=SDS((m, d), F32),
        scratch_shapes=[pltpu.VMEM((tm, d), BF16)],
        compiler_params=_cparams(("parallel", "arbitrary")),
        name="ffn",
    )(x, g, w1, w3, w2)


def _out_proj_body(x_ref, a_ref, b_ref, wa_ref, wb_ref, o_ref):
    o_ref[...] = x_ref[...] + _dot(a_ref[...], wa_ref[...]) + _dot(b_ref[...], wb_ref[...])


def out_proj(x, a, b, wa, wb, tm):
    m, d = x.shape
    ka, kb = a.shape[1], b.shape[1]
    return pl.pallas_call(
        _out_proj_body,
        grid=(m // tm,),
        in_specs=[
            pl.BlockSpec((tm, d), lambda i: (i, 0)),
            pl.BlockSpec((tm, ka), lambda i: (i, 0)),
            pl.BlockSpec((tm, kb), lambda i: (i, 0)),
            pl.BlockSpec((ka, d), lambda i: (0, 0)),
            pl.BlockSpec((kb, d), lambda i: (0, 0)),
        ],
        out_specs=pl.BlockSpec((tm, d), lambda i: (i, 0)),
        out_shape=SDS((m, d), F32),
        compiler_params=_cparams(("parallel",)),
        name="out_proj",
    )(x, a, b, wa, wb)


S5_SLABS = 8
S5_ROWS = 32


def _s5_body(u_ref, h0_ref, lam_ref, bw_ref, cw_ref, d_ref, wg_ref, bg_ref,
             o_ref, hl_ref, bu_ref, hs_ref, carry_ref, *, tm, n_seq, seq_len):
    first_tile = pl.program_id(1) == 0
    u = u_ref[...]
    ub = u.astype(BF16)
    nr = 2 * S5_ROWS
    for j in range(S5_SLABS):
        res = _dot(ub[:, LANES * j:LANES * (j + 1)], bw_ref[j])
        for q in range(4):
            bu_ref[pl.ds(4 * j + q, tm, stride=nr), :] = res[:, LANES * q:LANES * (q + 1)]
            bu_ref[pl.ds(S5_ROWS + 4 * j + q, tm, stride=nr), :] = res[:, 512 + LANES * q:512 + LANES * (q + 1)]

    lr = lam_ref[0]
    li = lam_ref[1]

    def seq_body(s, _):
        hr = jnp.where(first_tile, h0_ref[s, 0], carry_ref[0])
        hi = jnp.where(first_tile, h0_ref[s, 1], carry_ref[1])

        def step(t, c):
            hr, hi = c
            row = pl.multiple_of((s * seq_len + t) * nr, nr)
            br = bu_ref[pl.ds(row, S5_ROWS), :]
            bi = bu_ref[pl.ds(row + S5_ROWS, S5_ROWS), :]
            nhr = lr * hr - li * hi + br
            nhi = lr * hi + li * hr + bi
            hs_ref[pl.ds(row, S5_ROWS), :] = nhr
            hs_ref[pl.ds(row + S5_ROWS, S5_ROWS), :] = nhi
            return nhr, nhi

        hr, hi = lax.fori_loop(0, seq_len, step, (hr, hi), unroll=min(seq_len, 8))
        carry_ref[0] = hr
        carry_ref[1] = hi
        hl_ref[s, 0] = hr
        hl_ref[s, 1] = hi
        return 0

    lax.fori_loop(0, n_seq, seq_body, 0)

    ys = []
    for j in range(S5_SLABS):
        parts = [hs_ref[pl.ds(4 * j + q, tm, stride=nr), :] for q in range(4)]
        parts += [hs_ref[pl.ds(S5_ROWS + 4 * j + q, tm, stride=nr), :] for q in range(4)]
        lhs = jnp.concatenate(parts, axis=1).astype(BF16)
        ys.append(_dot(lhs, cw_ref[j]))
    y = jnp.concatenate(ys, axis=1) + d_ref[...] * u
    z = jax.nn.gelu(y)
    gate = jax.nn.sigmoid(_dot(z.astype(BF16), wg_ref[...]) + bg_ref[...])
    o_ref[...] = (z * gate).astype(o_ref.dtype)


def s5_mix(proj, h0, lam, bw, cw, d, wg, bg, *, n_groups, tiles_per_group, tm, n_seq, seq_len):
    width = S5_SLABS * LANES
    m = proj.shape[0]
    kern = functools.partial(_s5_body, tm=tm, n_seq=n_seq, seq_len=seq_len)
    const3 = lambda g, t: (0, 0, 0)
    return pl.pallas_call(
        kern,
        grid=(n_groups, tiles_per_group),
        in_specs=[
            pl.BlockSpec((tm, width), lambda g, t: (g * tiles_per_group + t, 0)),
            pl.BlockSpec((n_seq, 2, S5_ROWS, LANES), lambda g, t: (g, 0, 0, 0)),
            pl.BlockSpec((2, S5_ROWS, LANES), const3),
            pl.BlockSpec((S5_SLABS, LANES, 1024), const3),
            pl.BlockSpec((S5_SLABS, 1024, LANES), const3),
            pl.BlockSpec((1, width), lambda g, t: (0, 0)),
            pl.BlockSpec((width, width), lambda g, t: (0, 0)),
            pl.BlockSpec((1, width), lambda g, t: (0, 0)),
        ],
        out_specs=[
            pl.BlockSpec((tm, width), lambda g, t: (g * tiles_per_group + t, 0)),
            pl.BlockSpec((n_seq, 2, S5_ROWS, LANES), lambda g, t: (g, 0, 0, 0)),
        ],
        out_shape=[SDS((m, width), BF16), SDS(h0.shape, F32)],
        scratch_shapes=[
            pltpu.VMEM((tm * 2 * S5_ROWS, LANES), F32),
            pltpu.VMEM((tm * 2 * S5_ROWS, LANES), F32),
            pltpu.VMEM((2, S5_ROWS, LANES), F32),
        ],
        compiler_params=_cparams(("parallel", "arbitrary")),
        name="s5_mix",
    )(proj, h0, lam, bw, cw, d, wg, bg)


def _pair_swap(y, half):
    n = y.shape[-1]
    lane = lax.broadcasted_iota(jnp.int32, y.shape, y.ndim - 1)
    up = pltpu.roll(y, n - half, y.ndim - 1)
    dn = pltpu.roll(y, half, y.ndim - 1)
    return jnp.where((lane % (2 * half)) < half, up, dn)


def _mla_prep_body(qn_ref, qr_ref, cl_ref, kr_ref, cos_ref, sin_ref, gq_ref, gqr_ref, glat_ref, gkr_ref,
                   wukt_ref, wuk_ref, grp_ref, sel_ref, selt_ref,
                   qlat_ref, qrope_ref, klat_ref, klatb_ref, krope_ref, kropeb_ref, krs_ref, krst_ref):
    half = D_ROPE // 2
    cos = cos_ref[...]
    sin = sin_ref[...]
    for h in range(MLA_HEADS):
        x = qn_ref[:, D_NOPE * h:D_NOPE * (h + 1)]
        y = _rms(x, gq_ref[...])
        ql = _dot(y.astype(BF16), wukt_ref[h]) * MLA_SCALE
        qlat_ref[:, D_LATENT * h:D_LATENT * (h + 1)] = ql.astype(BF16)
    x = qr_ref[...]
    hi, lo = _split_bf16(x * x)
    ms = (_dot(hi, grp_ref[...]) + _dot(lo, grp_ref[...])) * (1.0 / D_ROPE)
    y = x * lax.rsqrt(ms + EPS) * gqr_ref[...]
    cos4 = jnp.concatenate([cos] * 4, axis=1)
    sin4 = jnp.concatenate([sin] * 4, axis=1)
    qrope_ref[...] = ((y * cos4 + _pair_swap(y, half) * sin4) * MLA_SCALE).astype(BF16)
    cl = _rms(cl_ref[...], glat_ref[...])
    klat_ref[...] = cl
    clb = cl.astype(BF16)
    klatb_ref[...] = clb
    kf = _dot(clb, wuk_ref[...])
    hi, lo = _split_bf16(kf * kf)
    ss = _dot(hi, sel_ref[...]) + _dot(lo, sel_ref[...])
    krs_ref[...] = lax.rsqrt(ss[:, 0:MLA_HEADS] * (1.0 / D_NOPE) + EPS)
    sst = _dot_nt(selt_ref[...], hi) + _dot_nt(selt_ref[...], lo)
    krst_ref[...] = lax.rsqrt(sst * (1.0 / D_NOPE) + EPS)
    x = kr_ref[...]
    ms = jnp.sum(x * x, axis=-1, keepdims=True) * (1.0 / LANES)
    y = x * lax.rsqrt(ms + EPS) * gkr_ref[...]
    y = y * cos + _pair_swap(y, half) * sin
    krope_ref[...] = y[:, 0:D_ROPE]
    lane = lax.broadcasted_iota(jnp.int32, y.shape, 1)
    yb = y.astype(BF16)
    zero = jnp.zeros_like(yb)
    kropeb_ref[:, 0:LANES] = jnp.where(lane < D_ROPE, yb, zero)
    kropeb_ref[:, LANES:2 * LANES] = jnp.where(lane >= D_ROPE, yb, zero)


def mla_prep(proj, cos, sin, gq, gqr, glat, gkr, wukt, wuk, grp, sel, selt, tm):
    m = proj.shape[0]
    h = MLA_HEADS
    row = lambda i: (i, 0)
    c2 = lambda i: (0, 0)
    c3 = lambda i: (0, 0, 0)
    return pl.pallas_call(
        _mla_prep_body,
        grid=(m // tm,),
        in_specs=[
            pl.BlockSpec((tm, 1024), lambda i: (i, 1)),
            pl.BlockSpec((tm, 512), lambda i: (i, 4)),
            pl.BlockSpec((tm, 512), lambda i: (i, 5)),
            pl.BlockSpec((tm, LANES), lambda i: (i, 24)),
            pl.BlockSpec((tm, LANES), row),
            pl.BlockSpec((tm, LANES), row),
            pl.BlockSpec((1, D_NOPE), c2),
            pl.BlockSpec((1, 512), c2),
            pl.BlockSpec((1, D_LATENT), c2),
            pl.BlockSpec((1, LANES), c2),
            pl.BlockSpec((h, D_NOPE, D_LATENT), c3),
            pl.BlockSpec((D_LATENT, h * D_NOPE), c2),
            pl.BlockSpec((512, 512), c2),
            pl.BlockSpec((h * D_NOPE, LANES), c2),
            pl.BlockSpec((h, h * D_NOPE), c2),
        ],
        out_specs=[
            pl.BlockSpec((tm, h * D_LATENT), row),
            pl.BlockSpec((tm, 512), row),
            pl.BlockSpec((tm, D_LATENT), row),
            pl.BlockSpec((tm, D_LATENT), row),
            pl.BlockSpec((tm, D_ROPE), row),
            pl.BlockSpec((tm, 2 * LANES), row),
            pl.BlockSpec((tm, h), row),
            pl.BlockSpec((h, tm), lambda i: (0, i)),
        ],
        out_shape=[
            SDS((m, h * D_LATENT), BF16),
            SDS((m, 512), BF16),
            SDS((m, D_LATENT), F32),
            SDS((m, D_LATENT), BF16),
            SDS((m, D_ROPE), F32),
            SDS((m, 2 * LANES), BF16),
            SDS((m, h), F32),
            SDS((h, m), F32),
        ],
        compiler_params=_cparams(("parallel",)),
        name="mla_prep",
    )(proj, proj, proj, proj, cos, sin, gq, gqr, glat, gkr, wukt, wuk, grp, sel, selt)


def _attn_prompt_body(qi_ref, ki_ref, ql_ref, qr_ref, kl_ref, kr_ref, rst_ref, wuv_ref,
                      o_ref, m_ref, l_ref, acc_ref, *, tq):
    s_idx = pl.program_id(1)
    qi = qi_ref[s_idx]
    ki = ki_ref[s_idx]

    @pl.when(ki == 0)
    def _():
        m_ref[...] = jnp.full(m_ref.shape, NEG, F32)
        l_ref[...] = jnp.zeros(l_ref.shape, F32)
        acc_ref[...] = jnp.zeros(acc_ref.shape, F32)

    kl = kl_ref[...]
    rows = lax.broadcasted_iota(jnp.int32, (tq, tq), 0)
    cols = lax.broadcasted_iota(jnp.int32, (tq, tq), 1)
    keep = jnp.logical_or(ki < qi, cols <= rows)
    for h in range(MLA_HEADS):
        j, half = h % 4, h // 4
        s = _dot_nt(ql_ref[:, D_LATENT * h:D_LATENT * (h + 1)], kl) * rst_ref[h:h + 1, :]
        s = s + _dot_nt(qr_ref[:, LANES * j:LANES * (j + 1)], kr_ref[:, LANES * half:LANES * (half + 1)])
        s = jnp.where(keep, s, NEG)
        m_old = m_ref[h]
        m_new = jnp.maximum(m_old, jnp.max(s, axis=-1, keepdims=True))
        p = jnp.exp(s - m_new)
        alpha = jnp.exp(m_old - m_new)
        l_ref[h] = alpha * l_ref[h] + jnp.sum(p, axis=-1, keepdims=True)
        acc_ref[h] = alpha * acc_ref[h] + _dot(p.astype(BF16), kl)
        m_ref[h] = m_new

    @pl.when(ki == qi)
    def _():
        for h in range(MLA_HEADS):
            o_lat = acc_ref[h] / l_ref[h]
            o_ref[:, D_V * h:D_V * (h + 1)] = _dot(o_lat.astype(BF16), wuv_ref[h]).astype(o_ref.dtype)


def attn_prompt(qlat, qrope, klat, krope, krst, wuv, *, batch, seq, tq):
    nq = seq // tq
    qi_tab = np.concatenate([np.full(q + 1, q, np.int32) for q in range(nq)])
    ki_tab = np.concatenate([np.arange(q + 1, dtype=np.int32) for q in range(nq)])
    n_tri = len(qi_tab)
    h = MLA_HEADS
    kern = functools.partial(_attn_prompt_body, tq=tq)
    grid_spec = pltpu.PrefetchScalarGridSpec(
        num_scalar_prefetch=2,
        grid=(batch, n_tri),
        in_specs=[
            pl.BlockSpec((tq, h * D_LATENT), lambda b, s, qi, ki: (b * nq + qi[s], 0)),
            pl.BlockSpec((tq, 512), lambda b, s, qi, ki: (b * nq + qi[s], 0)),
            pl.BlockSpec((tq, D_LATENT), lambda b, s, qi, ki: (b * nq + ki[s], 0)),
            pl.BlockSpec((tq, 2 * LANES), lambda b, s, qi, ki: (b * nq + ki[s], 0)),
            pl.BlockSpec((h, tq), lambda b, s, qi, ki: (0, b * nq + ki[s])),
            pl.BlockSpec((h, D_LATENT, D_V), lambda b, s, qi, ki: (0, 0, 0)),
        ],
        out_specs=pl.BlockSpec((tq, h * D_V), lambda b, s, qi, ki: (b * nq + qi[s], 0)),
        scratch_shapes=[
            pltpu.VMEM((h, tq, 1), F32),
            pltpu.VMEM((h, tq, 1), F32),
            pltpu.VMEM((h, tq, D_LATENT), F32),
        ],
    )
    return pl.pallas_call(
        kern,
        grid_spec=grid_spec,
        out_shape=SDS((batch * seq, h * D_V), BF16),
        compiler_params=_cparams(("parallel", "arbitrary")),
        name="attn_prompt",
    )(jnp.asarray(qi_tab), jnp.asarray(ki_tab), qlat, qrope, klat, krope, krst, wuv)


def _attn_sample_body(pt_ref, ql_ref, qr_ref, *refs, n_pages_step, dec_seq):
    p_n = n_pages_step
    lat_refs = refs[0:p_n]
    rope_refs = refs[p_n:2 * p_n]
    rs_refs = refs[2 * p_n:3 * p_n]
    klo_ref, kro_ref, rso_ref, o_ref, kpad_ref, m_ref, l_ref, acc_ref = refs[3 * p_n:]
    j = pl.program_id(1)
    rows = dec_seq * MLA_HEADS

    @pl.when(j == 0)
    def _():
        m_ref[...] = jnp.full(m_ref.shape, NEG, F32)
        l_ref[...] = jnp.zeros(l_ref.shape, F32)
        acc_ref[...] = jnp.zeros(acc_ref.shape, F32)
        kpad_ref[...] = jnp.zeros(kpad_ref.shape, BF16)

    ql = ql_ref[...]
    qr = qr_ref[...]

    def update(s, kl):
        m_old = m_ref[...]
        m_new = jnp.maximum(m_old, jnp.max(s, axis=-1, keepdims=True))
        p = jnp.exp(s - m_new)
        alpha = jnp.exp(m_old - m_new)
        l_ref[...] = alpha * l_ref[...] + jnp.sum(p, axis=-1, keepdims=True)
        acc_ref[...] = alpha * acc_ref[...] + _dot(p.astype(BF16), kl)
        m_ref[...] = m_new

    kl = jnp.concatenate([r[...].astype(BF16) for r in lat_refs], axis=0)
    for i in range(p_n):
        kpad_ref[PAGE * i:PAGE * (i + 1), 0:D_ROPE] = rope_refs[i][...].astype(BF16)
    rst = jnp.concatenate([r[...].T for r in rs_refs], axis=1)
    tk = p_n * PAGE
    s = _dot_nt(ql, kl)
    s = (s.reshape(dec_seq, MLA_HEADS, tk) * rst[None]).reshape(rows, tk)
    s = s + _dot_nt(qr, kpad_ref[...])
    update(s, kl)

    @pl.when(j == pl.num_programs(1) - 1)
    def _():
        klo = klo_ref[...]
        s = _dot_nt(ql, klo)
        s = (s.reshape(dec_seq, MLA_HEADS, PAGE) * rso_ref[...][None]).reshape(rows, PAGE)
        s = s + _dot_nt(qr, kro_ref[...])
        tok = lax.broadcasted_iota(jnp.int32, (rows, PAGE), 0) // MLA_HEADS
        key = lax.broadcasted_iota(jnp.int32, (rows, PAGE), 1)
        s = jnp.where(key <= tok, s, NEG)
        update(s, klo)
        o_ref[...] = (acc_ref[...] / l_ref[...]).astype(o_ref.dtype)


def attn_sample(page_table, qlat, qrope, cache_lat, cache_rope, cache_rs, klat_own, krope_own, rst_own,
                *, layer, n_pages_step):
    n_seq, n_pages = page_table.shape
    rows = qlat.shape[1]
    dec_seq = rows // MLA_HEADS
    p_n = n_pages_step
    n_steps = n_pages // p_n
    kern = functools.partial(_attn_sample_body, n_pages_step=p_n, dec_seq=dec_seq)

    def page_spec(width, i):
        return pl.BlockSpec((None, None, PAGE, width),
                            lambda s, j, pt: (layer, pt[s * n_pages + j * p_n + i], 0, 0))

    own = lambda s, j, pt: (s, 0, 0)
    in_specs = [
        pl.BlockSpec((None, rows, D_LATENT), own),
        pl.BlockSpec((None, rows, LANES), own),
    ]
    in_specs += [page_spec(D_LATENT, i) for i in range(p_n)]
    in_specs += [page_spec(D_ROPE, i) for i in range(p_n)]
    in_specs += [page_spec(MLA_HEADS, i) for i in range(p_n)]
    in_specs += [
        pl.BlockSpec((None, PAGE, D_LATENT), own),
        pl.BlockSpec((None, PAGE, LANES), own),
        pl.BlockSpec((None, MLA_HEADS, PAGE), own),
    ]
    grid_spec = pltpu.PrefetchScalarGridSpec(
        num_scalar_prefetch=1,
        grid=(n_seq, n_steps),
        in_specs=in_specs,
        out_specs=pl.BlockSpec((None, rows, D_LATENT), own),
        scratch_shapes=[
            pltpu.VMEM((p_n * PAGE, LANES), BF16),
            pltpu.VMEM((rows, 1), F32),
            pltpu.VMEM((rows, 1), F32),
            pltpu.VMEM((rows, D_LATENT), F32),
        ],
    )
    args = [page_table.reshape(-1), qlat, qrope]
    args += [cache_lat] * p_n + [cache_rope] * p_n + [cache_rs] * p_n
    args += [klat_own, krope_own, rst_own]
    return pl.pallas_call(
        kern,
        grid_spec=grid_spec,
        out_shape=SDS((n_seq, rows, D_LATENT), BF16),
        compiler_params=_cparams(("parallel", "arbitrary")),
        name="attn_sample",
    )(*args)


def _head_matmul_body(x_ref, w_ref, o_ref):
    o_ref[...] = _dot(x_ref[...], w_ref[...]).astype(o_ref.dtype)


def head_matmul(x, w):
    m = x.shape[0]
    h, k, n = w.shape
    return pl.pallas_call(
        _head_matmul_body,
        grid=(h,),
        in_specs=[pl.BlockSpec((m, k), lambda i: (0, i)), pl.BlockSpec((None, k, n), lambda i: (i, 0, 0))],
        out_specs=pl.BlockSpec((m, n), lambda i: (0, i)),
        out_shape=SDS((m, h * n), BF16),
        compiler_params=_cparams(("parallel",)),
        name="head_matmul",
    )(x, w)


def _cumsum_rows(x, c):
    r = lax.broadcasted_iota(jnp.int32, (c, c), 0)
    s = lax.broadcasted_iota(jnp.int32, (c, c), 1)
    tri = (s <= r).astype(BF16)
    hi, lo = _split_bf16(x)
    return _dot(tri, hi) + _dot(tri, lo)


def _gla_body(q_ref, k_ref, v_ref, gg_ref, sm_ref, wa_ref, ba_ref, gn_ref, s0_ref,
              o_ref, so_ref, st_ref, b_ref, oi_ref, *, c):
    ci = pl.program_id(2)

    @pl.when(ci == 0)
    def _():
        st_ref[...] = s0_ref[...].T

    nb = c // SUBLANES
    x = _dot(sm_ref[...].astype(BF16), wa_ref[...]) + ba_ref[...]
    g = _log_sigmoid(x) * (1.0 / GLA_GATE_NORM)
    b = _cumsum_rows(g, c)
    b_ref[...] = b
    q = q_ref[...]
    k = k_ref[...]
    v = v_ref[...]
    st = st_ref[...]
    o_inter = _dot_nt((q * jnp.exp(b)).astype(BF16), st.astype(BF16))

    sub = lax.broadcasted_iota(jnp.int32, (SUBLANES, 1), 0)

    def block_i(bi, _):
        r0 = pl.multiple_of(bi * SUBLANES, SUBLANES)

        def block_j(bj, accs):
            c0 = pl.multiple_of(bj * SUBLANES, SUBLANES)
            kj = k_ref[pl.ds(c0, SUBLANES), :]
            bj_rows = b_ref[pl.ds(c0, SUBLANES), :]
            vj = v_ref[pl.ds(c0, SUBLANES), :]
            new = []
            for tt in range(SUBLANES):
                qt = q_ref[pl.ds(r0 + tt, 1), :]
                bt = b_ref[pl.ds(r0 + tt, 1), :]
                dlt = jnp.where(c0 + sub <= r0 + tt, bt - bj_rows, NEG)
                a = jnp.sum(jnp.exp(dlt) * kj * qt, axis=-1, keepdims=True)
                new.append(accs[tt] + a * vj)
            return tuple(new)

        zero = jnp.zeros((SUBLANES, GLA_DV), F32)
        accs = lax.fori_loop(0, bi + 1, block_j, (zero,) * SUBLANES)
        for tt in range(SUBLANES):
            oi_ref[pl.ds(r0 + tt, 1), :] = jnp.sum(accs[tt], axis=0, keepdims=True)
        return 0

    lax.fori_loop(0, nb, block_i, 0)

    o = (o_inter + oi_ref[...]) * (GLA_DK ** -0.5)
    o = _rms(o, gn_ref[...])
    gg = gg_ref[...]
    o_ref[...] = (o * (gg * jax.nn.sigmoid(gg))).astype(o_ref.dtype)

    b_last = b[c - 1:c, :]
    kd = k * jnp.exp(b_last - b)
    st_new = st * jnp.exp(b_last) + _dot_tn(v.astype(BF16), kd.astype(BF16))
    st_ref[...] = st_new

    @pl.when(ci == pl.num_programs(2) - 1)
    def _():
        so_ref[...] = st_new.T


def gla_mix(proj, wa, ba, gn, s0, *, n_seq, n_chunks, c):
    m = proj.shape[0]
    hh = GLA_HEADS
    kern = functools.partial(_gla_body, c=c)
    row = lambda s, h, ci: s * n_chunks + ci
    return pl.pallas_call(
        kern,
        grid=(n_seq, hh, n_chunks),
        in_specs=[
            pl.BlockSpec((c, GLA_DK), lambda s, h, ci: (row(s, h, ci), h)),
            pl.BlockSpec((c, GLA_DK), lambda s, h, ci: (row(s, h, ci), 4 + h)),
            pl.BlockSpec((c, GLA_DV), lambda s, h, ci: (row(s, h, ci), 4 + h)),
            pl.BlockSpec((c, GLA_DV), lambda s, h, ci: (row(s, h, ci), 8 + h)),
            pl.BlockSpec((c, LANES), lambda s, h, ci: (row(s, h, ci), 48)),
            pl.BlockSpec((LANES, GLA_DK), lambda s, h, ci: (0, h)),
            pl.BlockSpec((1, GLA_DK), lambda s, h, ci: (0, h)),
            pl.BlockSpec((1, GLA_DV), lambda s, h, ci: (0, 0)),
            pl.BlockSpec((None, None, GLA_DK, GLA_DV), lambda s, h, ci: (s, h, 0, 0)),
        ],
        out_specs=[
            pl.BlockSpec((c, GLA_DV), lambda s, h, ci: (row(s, h, ci), h)),
            pl.BlockSpec((None, None, GLA_DK, GLA_DV), lambda s, h, ci: (s, h, 0, 0)),
        ],
        out_shape=[SDS((m, hh * GLA_DV), BF16), SDS((n_seq, hh, GLA_DK, GLA_DV), F32)],
        scratch_shapes=[
            pltpu.VMEM((GLA_DV, GLA_DK), F32),
            pltpu.VMEM((c, GLA_DK), F32),
            pltpu.VMEM((c, GLA_DV), F32),
        ],
        compiler_params=_cparams(("parallel", "parallel", "arbitrary")),
        name="gla_mix",
    )(proj, proj, proj, proj, proj, wa, ba, gn, s0)


ML_I_LANE = GLA_RANK
ML_F_LANE = GLA_RANK + ML_HEADS


def _mlstm_body(q_ref, k_ref, v_ref, mo_ref, sm_ref, bias_ref, gn_ref, c0_ref, n0_ref, m0_ref,
                o_ref, co_ref, no_ref, mo_out_ref, cs_ref, ns_ref, ms_ref, tr_ref, *, c):
    h = pl.program_id(1)
    ci = pl.program_id(2)

    @pl.when(ci == 0)
    def _():
        cs_ref[...] = c0_ref[...]
        ns_ref[...] = n0_ref[...]
        ms_ref[...] = m0_ref[...]

    pre = sm_ref[...] + bias_ref[...]
    lane = lax.broadcasted_iota(jnp.int32, pre.shape, 1)
    bcum = _cumsum_rows(_log_sigmoid(pre), c)
    i_col = jnp.sum(jnp.where(lane == ML_I_LANE + h, pre, 0.0), axis=-1, keepdims=True)
    b_col = jnp.sum(jnp.where(lane == ML_F_LANE + h, bcum, 0.0), axis=-1, keepdims=True)
    tr_ref[0] = pre.T
    tr_ref[1] = bcum.T
    i_row = tr_ref[0, pl.ds(ML_I_LANE + h, 1), :]
    b_row = tr_ref[1, pl.ds(ML_F_LANE + h, 1), :]

    m_prev = ms_ref[...]
    r = lax.broadcasted_iota(jnp.int32, (c, c), 0)
    s = lax.broadcasted_iota(jnp.int32, (c, c), 1)
    rel = jnp.where(s <= r, b_col - b_row + i_row, NEG)
    carry_log = b_col + m_prev
    m_t = jnp.maximum(carry_log, jnp.max(rel, axis=-1, keepdims=True))
    w_intra = jnp.exp(rel - m_t)
    w_carry = jnp.exp(carry_log - m_t)

    q = q_ref[...] * (ML_DK ** -0.5)
    k = k_ref[...]
    v = v_ref[...]
    qb = q.astype(BF16)
    vb = v.astype(BF16)
    cm = cs_ref[...]
    nv = ns_ref[...]
    qk = _dot_nt(qb, k.astype(BF16)) * w_intra
    num = _dot(qk.astype(BF16), vb) + w_carry * _dot(qb, cm.astype(BF16))
    den = jnp.sum(qk, axis=-1, keepdims=True) + w_carry * jnp.sum(q * nv, axis=-1, keepdims=True)
    hh = num / jnp.maximum(jnp.abs(den), jnp.exp(-m_t))

    mo = mo_ref[...]
    o_ref[...] = (jax.nn.sigmoid(mo) * _rms(hh, gn_ref[...])).astype(o_ref.dtype)

    m_new = m_t[c - 1:c, :]
    b_last = b_col[c - 1:c, :]
    w_state = jnp.exp(b_last - b_col + i_col - m_new)
    decay = jnp.exp(b_last + m_prev - m_new)
    kw = k * w_state
    cm_new = decay * cm + _dot_tn(kw.astype(BF16), vb)
    nv_new = decay * nv + jnp.sum(kw, axis=0, keepdims=True)
    cs_ref[...] = cm_new
    ns_ref[...] = nv_new
    ms_ref[...] = m_new

    @pl.when(ci == pl.num_programs(2) - 1)
    def _():
        co_ref[...] = cm_new
        no_ref[...] = nv_new
        mo_out_ref[...] = m_new


def mlstm_mix(proj, bias, gn, c0, n0, m0, *, n_seq, n_chunks, c):
    m = proj.shape[0]
    hh = ML_HEADS
    kern = functools.partial(_mlstm_body, c=c)
    row = lambda s, h, ci: s * n_chunks + ci
    st4 = lambda s, h, ci: (s, h, 0, 0)
    return pl.pallas_call(
        kern,
        grid=(n_seq, hh, n_chunks),
        in_specs=[
            pl.BlockSpec((c, ML_DK), lambda s, h, ci: (row(s, h, ci), 24 + h)),
            pl.BlockSpec((c, ML_DK), lambda s, h, ci: (row(s, h, ci), 28 + h)),
            pl.BlockSpec((c, ML_DV), lambda s, h, ci: (row(s, h, ci), 16 + h)),
            pl.BlockSpec((c, ML_DV), lambda s, h, ci: (row(s, h, ci), 20 + h)),
            pl.BlockSpec((c, LANES), lambda s, h, ci: (row(s, h, ci), 48)),
            pl.BlockSpec((1, LANES), lambda s, h, ci: (0, 0)),
            pl.BlockSpec((1, ML_DV), lambda s, h, ci: (0, 0)),
            pl.BlockSpec((None, None, ML_DK, ML_DV), st4),
            pl.BlockSpec((None, None, 1, ML_DK), st4),
            pl.BlockSpec((None, None, 1, 1), st4),
        ],
        out_specs=[
            pl.BlockSpec((c, ML_DV), lambda s, h, ci: (row(s, h, ci), h)),
            pl.BlockSpec((None, None, ML_DK, ML_DV), st4),
            pl.BlockSpec((None, None, 1, ML_DK), st4),
            pl.BlockSpec((None, None, 1, 1), st4),
        ],
        out_shape=[
            SDS((m, hh * ML_DV), BF16),
            SDS((n_seq, hh, ML_DK, ML_DV), F32),
            SDS((n_seq, hh, 1, ML_DK), F32),
            SDS((n_seq, hh, 1, 1), F32),
        ],
        scratch_shapes=[
            pltpu.VMEM((ML_DK, ML_DV), F32),
            pltpu.VMEM((1, ML_DK), F32),
            pltpu.VMEM((1, 1), F32),
            pltpu.VMEM((2, LANES, c), F32),
        ],
        compiler_params=_cparams(("parallel", "parallel", "arbitrary")),
        name="mlstm_mix",
    )(proj, proj, proj, proj, proj, bias, gn, c0, n0, m0)


def _pad_cols(w, n):
    return jnp.pad(w, ((0, 0), (0, n - w.shape[1])))


def _even_in_weight(w):
    d = w.shape[0]
    hq = MLA_HEADS * (D_NOPE + D_ROPE)
    u = w[:, :1024]
    q = w[:, 1024:1024 + hq].reshape(d, MLA_HEADS, D_NOPE + D_ROPE)
    q_nope = q[:, :, :D_NOPE].reshape(d, MLA_HEADS * D_NOPE)
    q_rope = q[:, :, D_NOPE:].reshape(d, 2, 4, D_ROPE).transpose(0, 2, 1, 3).reshape(d, MLA_HEADS * D_ROPE)
    c_lat = w[:, 1024 + hq:1024 + hq + D_LATENT]
    k_r = w[:, 1024 + hq + D_LATENT:]
    return _pad_cols(jnp.concatenate([u, q_nope, q_rope, c_lat, k_r, k_r], axis=1), 3328).astype(BF16)


def _odd_in_weight(w):
    sizes = (512, 512, 1024, GLA_RANK, 1024, 512, 512, 1024, ML_HEADS, ML_HEADS, 1024)
    offs = np.concatenate([[0], np.cumsum(sizes)])
    seg = [w[:, offs[i]:offs[i + 1]] for i in range(len(sizes))]
    gq, gk, gv, ga, gg, mq, mk, mv, mi, mf, mo = seg
    small = _pad_cols(jnp.concatenate([ga, mi, mf], axis=1), LANES)
    return _pad_cols(jnp.concatenate([gq, gk, gv, gg, mq, mk, mv, mo, small], axis=1), 6400).astype(BF16)


def _s5_params(a_re, a_im, log_dt, b_re, b_im, c_re, c_im):
    g, p = a_re.shape
    ar = jnp.minimum(a_re, -1e-4)
    ai = a_im
    dt = jnp.exp(log_dt)[:, None]
    mag = jnp.exp(ar * dt)
    lr = mag * jnp.cos(ai * dt)
    li = mag * jnp.sin(ai * dt)
    den = ar * ar + ai * ai
    cr = ((lr - 1.0) * ar + li * ai) / den
    ci = (li * ar - (lr - 1.0) * ai) / den
    bb_re = cr[..., None] * b_re - ci[..., None] * b_im
    bb_im = cr[..., None] * b_im + ci[..., None] * b_re
    eye = jnp.eye(8, dtype=F32)

    def in_slab(bb):
        x = bb.transpose(0, 2, 1).reshape(S5_SLABS, 8, S5_GROUP, p)
        return jnp.einsum("jaip,ab->jaibp", x, eye).reshape(S5_SLABS, 8 * S5_GROUP, 8 * p)

    def out_slab(cc):
        x = cc.transpose(0, 2, 1).reshape(S5_SLABS, 8, p, S5_GROUP)
        return jnp.einsum("japi,ab->japbi", x, eye).reshape(S5_SLABS, 8 * p, 8 * S5_GROUP)

    bw = jnp.concatenate([in_slab(bb_re), in_slab(bb_im)], axis=2).astype(BF16)
    cw = jnp.concatenate([out_slab(c_re), -out_slab(c_im)], axis=1).astype(BF16)
    lam = jnp.stack([lr.reshape(S5_ROWS, LANES), li.reshape(S5_ROWS, LANES)])
    return lam, bw, cw


def _rope_tables(pos):
    half = D_ROPE // 2
    inv_freq = ROPE_THETA ** (-jnp.arange(half, dtype=F32) / half)
    ang = pos.astype(F32)[:, None] * inv_freq
    cos, sin = jnp.cos(ang), jnp.sin(ang)
    return jnp.tile(cos, (1, 4)), jnp.tile(jnp.concatenate([-sin, sin], axis=1), (1, 2))


def _pick(n, prefs):
    for t in prefs:
        if n % t == 0:
            return t
    return n


def _even_layer(x, *, n_seq, seq_len, pos, h0_re, h0_im, attend, p):
    m = x.shape[0]
    tm = _pick(m, (512, 256))
    proj = norm_matmul(x, p["norm_mix"], p["w_in"], tm, 1664)
    if seq_len >= 256:
        tile = _pick(seq_len, (256,))
        s5_kw = dict(n_groups=n_seq, tiles_per_group=seq_len // tile, tm=tile, n_seq=1, seq_len=tile)
    else:
        per = max(1, 256 // seq_len)
        per = _pick(n_seq, (per, 16, 8, 4, 2, 1))
        s5_kw = dict(n_groups=n_seq // per, tiles_per_group=1, tm=per * seq_len, n_seq=per, seq_len=seq_len)
    h0 = jnp.stack([h0_re.reshape(n_seq, S5_ROWS, LANES), h0_im.reshape(n_seq, S5_ROWS, LANES)], axis=1)
    s5_y, h_last = s5_mix(proj, h0, p["lam"], p["bw"], p["cw"], p["s5_d"], p["w_glu"], p["b_glu"], **s5_kw)
    s5_re = h_last[:, 0].reshape(n_seq, S5_ROWS * LANES // S5_STATE, S5_STATE)
    s5_im = h_last[:, 1].reshape(n_seq, S5_ROWS * LANES // S5_STATE, S5_STATE)
    cos, sin = _rope_tables(pos)
    qlat, qrope, klat, klat_b, krope, krope_b, krs, krst = mla_prep(
        proj, cos, sin, p["g_q"], p["g_qr"], p["g_lat"], p["g_kr"], p["wukt"], p["wuk"], p["grp"], p["sel"],
        p["selt"], _pick(m, (256,)))
    o = attend(qlat, qrope, klat_b, krope_b, krst)
    x = out_proj(x, s5_y, o, p["w_out_a"], p["w_out_b"], tm)
    return x, (klat, krope, krs, s5_re, s5_im)


def _odd_layer(x, *, n_seq, seq_len, s0, c0, n0, m0, p):
    m = x.shape[0]
    tm = _pick(m, (512, 256))
    proj = norm_matmul(x, p["norm_mix"], p["w_in"], tm, 1280)
    c = math.gcd(seq_len, CHUNK)
    n_chunks = seq_len // c
    o_gla, s_gla = gla_mix(proj, p["wa"], p["ba"], p["g_gla"], s0, n_seq=n_seq, n_chunks=n_chunks, c=c)
    h_ml, cm, nv, mm = mlstm_mix(proj, p["ml_bias"], p["g_ml"], c0, n0[:, :, None, :], m0[:, :, None, None],
                                 n_seq=n_seq, n_chunks=n_chunks, c=c)
    x = out_proj(x, o_gla, h_ml, p["w_out_a"], p["w_out_b"], tm)
    return x, (s_gla, cm, nv[:, :, 0, :], mm[:, :, 0, 0])


def kernel(x_prompt, x_sample, cache_mla_latent, cache_mla_k_rope, cache_mla_k_rscale, state_s5_re, state_s5_im, state_gla, state_mlstm_c, state_mlstm_n, state_mlstm_m, page_table, norm_mix, norm_ffn, ffn_w1, ffn_w3, ffn_w2, e_w_in, s5_a_re, s5_a_im, s5_log_dt, s5_b_re, s5_b_im, s5_c_re, s5_c_im, s5_d, s5_w_glu, s5_b_glu, mla_g_qnope, mla_g_qrope, mla_g_knope, mla_g_krope, mla_g_latent, mla_w_uk, mla_w_uv, e_w_out, o_w_in, gla_w_a2, gla_b_a, gla_g_norm, ml_b_i, ml_b_f, ml_g_norm, o_w_out):
    bp, lp, d = x_prompt.shape
    bs, ls, _ = x_sample.shape
    depth = norm_mix.shape[0]
    past_len = page_table.shape[1] * PAGE
    pos_p = jnp.arange(lp, dtype=jnp.int32)
    pos_s = past_len + jnp.arange(ls, dtype=jnp.int32)
    xp = x_prompt.reshape(bp * lp, d)
    xs = x_sample.reshape(bs * ls, d)
    even_p, even_s, odd_p, odd_s = [], [], [], []
    for l in range(depth):
        j = l // 2
        if l % 2 == 0:
            lam, bw, cw = _s5_params(s5_a_re[j], s5_a_im[j], s5_log_dt[j], s5_b_re[j], s5_b_im[j],
                                     s5_c_re[j], s5_c_im[j])
            s5w = s5_a_re.shape[1] * S5_GROUP
            h = MLA_HEADS
            wuk = mla_w_uk[j]
            sel = (jnp.arange(h * D_NOPE)[:, None] // D_NOPE == jnp.arange(LANES)[None, :]).astype(BF16)
            gi = jnp.arange(512) // D_ROPE
            p = dict(
                norm_mix=norm_mix[l][None], w_in=_even_in_weight(e_w_in[j]),
                lam=lam, bw=bw, cw=cw, s5_d=s5_d[j][None], w_glu=s5_w_glu[j].astype(BF16), b_glu=s5_b_glu[j][None],
                g_q=(mla_g_qnope[j] * mla_g_knope[j])[None], g_qr=jnp.tile(mla_g_qrope[j], h)[None],
                g_lat=mla_g_latent[j][None], g_kr=jnp.tile(mla_g_krope[j], 2)[None],
                wukt=wuk.transpose(1, 2, 0).astype(BF16), wuk=wuk.reshape(D_LATENT, h * D_NOPE).astype(BF16),
                grp=(gi[:, None] == gi[None, :]).astype(BF16), sel=sel, selt=sel[:, :h].T,
                w_out_a=e_w_out[j][:s5w].astype(BF16), w_out_b=e_w_out[j][s5w:].astype(BF16),
            )
            wuv = mla_w_uv[j].transpose(1, 0, 2).astype(BF16)

            def attend_p(qlat, qrope, klat_b, krope_b, krst):
                return attn_prompt(qlat, qrope, klat_b, krope_b, krst, wuv, batch=bp, seq=lp, tq=_pick(lp, (256,)))

            def attend_s(qlat, qrope, klat_b, krope_b, krst):
                rows = ls * h
                qr = qrope.reshape(bs * ls, 4, 2, D_ROPE).transpose(0, 2, 1, 3).reshape(bs, rows, D_ROPE)
                qr = jnp.pad(qr, ((0, 0), (0, 0), (0, LANES - D_ROPE)))
                pad = ((0, 0), (0, PAGE - ls), (0, 0))
                klat_own = jnp.pad(klat_b.reshape(bs, ls, D_LATENT), pad)
                krope_own = jnp.pad(krope_b[:, :LANES].reshape(bs, ls, LANES), pad)
                rst_own = jnp.pad(krst.reshape(h, bs, ls).transpose(1, 0, 2), ((0, 0), (0, 0), (0, PAGE - ls)))
                n_pages = page_table.shape[1]
                o_lat = attn_sample(page_table, qlat.reshape(bs, rows, D_LATENT), qr, cache_mla_latent,
                                    cache_mla_k_rope, cache_mla_k_rscale, klat_own, krope_own, rst_own,
                                    layer=j, n_pages_step=_pick(n_pages, (8, 4, 2, 1)))
                return head_matmul(o_lat.reshape(bs * ls, h * D_LATENT), wuv)

            zeros = jnp.zeros((bp, s5_a_re.shape[1], S5_STATE), F32)
            xp, st_p = _even_layer(xp, n_seq=bp, seq_len=lp, pos=jnp.tile(pos_p, bp), h0_re=zeros, h0_im=zeros,
                                   attend=attend_p, p=p)
            xs, st_s = _even_layer(xs, n_seq=bs, seq_len=ls, pos=jnp.tile(pos_s, bs), h0_re=state_s5_re[j],
                                   h0_im=state_s5_im[j], attend=attend_s, p=p)
            even_p.append((st_p[0].reshape(bp, lp, -1), st_p[1].reshape(bp, lp, -1), st_p[2].reshape(bp, lp, -1),
                           st_p[3], st_p[4]))
            even_s.append((st_s[0].reshape(bs, ls, -1), st_s[1].reshape(bs, ls, -1), st_s[2].reshape(bs, ls, -1),
                           st_s[3], st_s[4]))
        else:
            hk = GLA_HEADS * GLA_DK
            wo = o_w_out[j]
            bias = jnp.zeros((LANES,), F32)
            bias = bias.at[ML_I_LANE:ML_I_LANE + ML_HEADS].set(ml_b_i[j]).at[ML_F_LANE:ML_F_LANE + ML_HEADS].set(ml_b_f[j])
            p = dict(
                norm_mix=norm_mix[l][None], w_in=_odd_in_weight(o_w_in[j]),
                wa=jnp.pad(gla_w_a2[j], ((0, LANES - GLA_RANK), (0, 0))).astype(BF16), ba=gla_b_a[j][None],
                g_gla=gla_g_norm[j][None], ml_bias=bias[None], g_ml=ml_g_norm[j][None],
                w_out_a=wo[:GLA_HEADS * GLA_DV].astype(BF16), w_out_b=wo[GLA_HEADS * GLA_DV:].astype(BF16),
            )
            del hk
            xp, st_p = _odd_layer(xp, n_seq=bp, seq_len=lp, s0=jnp.zeros((bp, GLA_HEADS, GLA_DK, GLA_DV), F32),
                                  c0=jnp.zeros((bp, ML_HEADS, ML_DK, ML_DV), F32),
                                  n0=jnp.zeros((bp, ML_HEADS, ML_DK), F32), m0=jnp.zeros((bp, ML_HEADS), F32), p=p)
            xs, st_s = _odd_layer(xs, n_seq=bs, seq_len=ls, s0=state_gla[j], c0=state_mlstm_c[j],
                                  n0=state_mlstm_n[j], m0=state_mlstm_m[j], p=p)
            odd_p.append(st_p)
            odd_s.append(st_s)
        w1 = ffn_w1[l].astype(BF16)
        w3 = ffn_w3[l].astype(BF16)
        w2 = ffn_w2[l].astype(BF16)
        g = norm_ffn[l][None]
        tf = _pick(w1.shape[1], (512, 256, 128))
        xp = ffn(xp, g, w1, w3, w2, _pick(xp.shape[0], (1024, 512, 256)), tf)
        xs = ffn(xs, g, w1, w3, w2, _pick(xs.shape[0], (1024, 512, 256)), tf)
    p_lat, p_rope, p_rscale, p_s5_re, p_s5_im = [jnp.stack(t) for t in zip(*even_p)]
    s_lat, s_rope, s_rscale, s_s5_re, s_s5_im = [jnp.stack(t) for t in zip(*even_s)]
    p_gla, p_ml_c, p_ml_n, p_ml_m = [jnp.stack(t) for t in zip(*odd_p)]
    s_gla, s_ml_c, s_ml_n, s_ml_m = [jnp.stack(t) for t in zip(*odd_s)]
    return (xp.reshape(bp, lp, d), xs.reshape(bs, ls, d),
            p_lat, p_rope, p_rscale, p_s5_re, p_s5_im, p_gla, p_ml_c, p_ml_n, p_ml_m,
            s_lat, s_rope, s_rscale, s_s5_re, s_s5_im, s_gla, s_ml_c, s_ml_n, s_ml_m)
```

```python
import functools
import math

import jax
import jax.numpy as jnp
import numpy as np
from jax import lax
from jax.experimental import pallas as pl
from jax.experimental.pallas import tpu as pltpu

F32, BF16 = jnp.float32, jnp.bfloat16
SDS = jax.ShapeDtypeStruct

EPS = 1e-6
LANES = 128
SUBLANES = 8
VMEM_LIMIT = 52 * 1024 * 1024

S5_GROUP = 16
S5_STATE = 64
MLA_HEADS = 8
D_NOPE = 128
D_ROPE = 64
D_V = 128
D_LATENT = 512
ROPE_THETA = 10000.0
MLA_SCALE = (D_NOPE + D_ROPE) ** -0.5
Q_SCALE = MLA_SCALE * math.log2(math.e)
PAGE = 128
GLA_HEADS = 4
GLA_DK = 128
GLA_DV = 256
GLA_RANK = 16
GLA_GATE_NORM = 16.0
ML_HEADS = 4
ML_DK = 128
ML_DV = 256
CHUNK = 64
NEG = -1e30


def _cparams(sem):
    return pltpu.CompilerParams(dimension_semantics=sem, vmem_limit_bytes=VMEM_LIMIT)


def _rms(x, g):
    ms = jnp.mean(x * x, axis=-1, keepdims=True)
    return x * lax.rsqrt(ms + EPS) * g


def _dot(a, b):
    return jnp.dot(a, b, preferred_element_type=F32)


def _dot_nt(a, b):
    return lax.dot_general(a, b, (((1,), (1,)), ((), ())), preferred_element_type=F32)


def _dot_tn(a, b):
    return lax.dot_general(a, b, (((0,), (0,)), ((), ())), preferred_element_type=F32)


def _split_bf16(x):
    hi = x.astype(BF16)
    lo = (x - hi.astype(F32)).astype(BF16)
    return hi, lo


def _log_sigmoid(x):
    return jnp.minimum(x, 0.0) - jnp.log(1.0 + jnp.exp(-jnp.abs(x)))


def _norm_matmul_body(x_ref, g_ref, w_ref, o_ref, xn_ref):
    @pl.when(pl.program_id(1) == 0)
    def _():
        xn_ref[...] = _rms(x_ref[...], g_ref[...]).astype(BF16)

    o_ref[...] = _dot(xn_ref[...], w_ref[...])


def norm_matmul(x, g, w, tm, tn):
    m, k = x.shape
    n = w.shape[1]
    return pl.pallas_call(
        _norm_matmul_body,
        grid=(m // tm, n // tn),
        in_specs=[
            pl.BlockSpec((tm, k), lambda i, j: (i, 0)),
            pl.BlockSpec((1, k), lambda i, j: (0, 0)),
            pl.BlockSpec((k, tn), lambda i, j: (0, j)),
        ],
        out_specs=pl.BlockSpec((tm, tn), lambda i, j: (i, j)),
        out_shape=SDS((m, n), F32),
        scratch_shapes=[pltpu.VMEM((tm, k), BF16)],
        compiler_params=_cparams(("parallel", "arbitrary")),
        name="norm_matmul",
    )(x, g, w)


def _ffn_body(x_ref, g_ref, w1_ref, w3_ref, w2_ref, o_ref, xn_ref):
    @pl.when(pl.program_id(1) == 0)
    def _():
        x = x_ref[...]
        xn_ref[...] = _rms(x, g_ref[...]).astype(BF16)
        o_ref[...] = x

    xn = xn_ref[...]
    a = _dot(xn, w1_ref[...])
    b = _dot(xn, w3_ref[...])
    h = (a * jax.nn.sigmoid(a) * b).astype(BF16)
    o_ref[...] += _dot(h, w2_ref[...])


def ffn(x, g, w1, w3, w2, tm, tf):
    m, d = x.shape
    f = w1.shape[1]
    return pl.pallas_call(
        _ffn_body,
        grid=(m // tm, f // tf),
        in_specs=[
            pl.BlockSpec((tm, d), lambda i, j: (i, 0), pipeline_mode=pl.Buffered(1)),
            pl.BlockSpec((1, d), lambda i, j: (0, 0)),
            pl.BlockSpec((d, tf), lambda i, j: (0, j)),
            pl.BlockSpec((d, tf), lambda i, j: (0, j)),
            pl.BlockSpec((tf, d), lambda i, j: (j, 0)),
        ],
        out_specs=pl.BlockSpec((tm, d), lambda i, j: (i, 0)),
        out_shape=SDS((m, d), F32),
        scratch_shapes=[pltpu.VMEM((tm, d), BF16)],
        compiler_params=_cparams(("parallel", "arbitrary")),
        name="ffn",
    )(x, g, w1, w3, w2)


def _out_proj_body(x_ref, a_ref, b_ref, wa_ref, wb_ref, o_ref):
    o_ref[...] = x_ref[...] + _dot(a_ref[...], wa_ref[...]) + _dot(b_ref[...], wb_ref[...])


def out_proj(x, a, b, wa, wb, tm):
    m, d = x.shape
    ka, kb = a.shape[1], b.shape[1]
    return pl.pallas_call(
        _out_proj_body,
        grid=(m // tm,),
        in_specs=[
            pl.BlockSpec((tm, d), lambda i: (i, 0)),
            pl.BlockSpec((tm, ka), lambda i: (i, 0)),
            pl.BlockSpec((tm, kb), lambda i: (i, 0)),
            pl.BlockSpec((ka, d), lambda i: (0, 0)),
            pl.BlockSpec((kb, d), lambda i: (0, 0)),
        ],
        out_specs=pl.BlockSpec((tm, d), lambda i: (i, 0)),
        out_shape=SDS((m, d), F32),
        compiler_params=_cparams(("parallel",)),
        name="out_proj",
    )(x, a, b, wa, wb)


S5_SLABS = 8
S5_ROWS = 32


def _s5_body(u_ref, h0_ref, lam_ref, bw_ref, cw_ref, d_ref, wg_ref, bg_ref,
             o_ref, hl_ref, bu_ref, hs_ref, carry_ref, *, tm, n_seq, seq_len):
    first_tile = pl.program_id(1) == 0
    u = u_ref[...]
    ub = u.astype(BF16)
    nr = 2 * S5_ROWS
    for j in range(S5_SLABS):
        res = _dot(ub[:, LANES * j:LANES * (j + 1)], bw_ref[j])
        for q in range(4):
            bu_ref[pl.ds(4 * j + q, tm, stride=nr), :] = res[:, LANES * q:LANES * (q + 1)]
            bu_ref[pl.ds(S5_ROWS + 4 * j + q, tm, stride=nr), :] = res[:, 512 + LANES * q:512 + LANES * (q + 1)]

    lr = lam_ref[0]
    li = lam_ref[1]

    def seq_body(s, _):
        hr = jnp.where(first_tile, h0_ref[s, 0], carry_ref[0])
        hi = jnp.where(first_tile, h0_ref[s, 1], carry_ref[1])

        def step(t, c):
            hr, hi = c
            row = pl.multiple_of((s * seq_len + t) * nr, nr)
            br = bu_ref[pl.ds(row, S5_ROWS), :]
            bi = bu_ref[pl.ds(row + S5_ROWS, S5_ROWS), :]
            nhr = lr * hr - li * hi + br
            nhi = lr * hi + li * hr + bi
            hs_ref[pl.ds(row, S5_ROWS), :] = nhr
            hs_ref[pl.ds(row + S5_ROWS, S5_ROWS), :] = nhi
            return nhr, nhi

        hr, hi = lax.fori_loop(0, seq_len, step, (hr, hi), unroll=min(seq_len, 8))
        carry_ref[0] = hr
        carry_ref[1] = hi
        hl_ref[s, 0] = hr
        hl_ref[s, 1] = hi
        return 0

    lax.fori_loop(0, n_seq, seq_body, 0)

    ys = []
    for j in range(S5_SLABS):
        parts = [hs_ref[pl.ds(4 * j + q, tm, stride=nr), :] for q in range(4)]
        parts += [hs_ref[pl.ds(S5_ROWS + 4 * j + q, tm, stride=nr), :] for q in range(4)]
        lhs = jnp.concatenate(parts, axis=1).astype(BF16)
        ys.append(_dot(lhs, cw_ref[j]))
    y = jnp.concatenate(ys, axis=1) + d_ref[...] * u
    z = jax.nn.gelu(y)
    gate = jax.nn.sigmoid(_dot(z.astype(BF16), wg_ref[...]) + bg_ref[...])
    o_ref[...] = (z * gate).astype(o_ref.dtype)


def s5_mix(proj, h0, lam, bw, cw, d, wg, bg, *, n_groups, tiles_per_group, tm, n_seq, seq_len):
    width = S5_SLABS * LANES
    m = proj.shape[0]
    kern = functools.partial(_s5_body, tm=tm, n_seq=n_seq, seq_len=seq_len)
    const3 = lambda g, t: (0, 0, 0)
    return pl.pallas_call(
        kern,
        grid=(n_groups, tiles_per_group),
        in_specs=[
            pl.BlockSpec((tm, width), lambda g, t: (g * tiles_per_group + t, 0)),
            pl.BlockSpec((n_seq, 2, S5_ROWS, LANES), lambda g, t: (g, 0, 0, 0)),
            pl.BlockSpec((2, S5_ROWS, LANES), const3),
            pl.BlockSpec((S5_SLABS, LANES, 1024), const3),
            pl.BlockSpec((S5_SLABS, 1024, LANES), const3),
            pl.BlockSpec((1, width), lambda g, t: (0, 0)),
            pl.BlockSpec((width, width), lambda g, t: (0, 0)),
            pl.BlockSpec((1, width), lambda g, t: (0, 0)),
        ],
        out_specs=[
            pl.BlockSpec((tm, width), lambda g, t: (g * tiles_per_group + t, 0)),
            pl.BlockSpec((n_seq, 2, S5_ROWS, LANES), lambda g, t: (g, 0, 0, 0)),
        ],
        out_shape=[SDS((m, width), BF16), SDS(h0.shape, F32)],
        scratch_shapes=[
            pltpu.VMEM((tm * 2 * S5_ROWS, LANES), F32),
            pltpu.VMEM((tm * 2 * S5_ROWS, LANES), F32),
            pltpu.VMEM((2, S5_ROWS, LANES), F32),
        ],
        compiler_params=_cparams(("parallel", "arbitrary")),
        name="s5_mix",
    )(proj, h0, lam, bw, cw, d, wg, bg)


def _pair_swap(y, half):
    n = y.shape[-1]
    lane = lax.broadcasted_iota(jnp.int32, y.shape, y.ndim - 1)
    up = pltpu.roll(y, n - half, y.ndim - 1)
    dn = pltpu.roll(y, half, y.ndim - 1)
    return jnp.where((lane % (2 * half)) < half, up, dn)


def _mla_prep_body(qn_ref, qr_ref, cl_ref, kr_ref, cos_ref, sin_ref, gq_ref, gqr_ref, glat_ref, gkr_ref,
                   wukt_ref, wuk_ref, grp_ref, sel_ref, selt_ref,
                   qlat_ref, qrope_ref, klat_ref, klatb_ref, krope_ref, kropeb_ref, krs_ref, krst_ref):
    half = D_ROPE // 2
    cos = cos_ref[...]
    sin = sin_ref[...]
    for h in range(MLA_HEADS):
        x = qn_ref[:, D_NOPE * h:D_NOPE * (h + 1)]
        y = _rms(x, gq_ref[...])
        ql = _dot(y.astype(BF16), wukt_ref[h]) * Q_SCALE
        qlat_ref[:, D_LATENT * h:D_LATENT * (h + 1)] = ql.astype(BF16)
    x = qr_ref[...]
    hi, lo = _split_bf16(x * x)
    ms = (_dot(hi, grp_ref[...]) + _dot(lo, grp_ref[...])) * (1.0 / D_ROPE)
    y = x * lax.rsqrt(ms + EPS) * gqr_ref[...]
    cos4 = jnp.concatenate([cos] * 4, axis=1)
    sin4 = jnp.concatenate([sin] * 4, axis=1)
    qrope_ref[...] = ((y * cos4 + _pair_swap(y, half) * sin4) * Q_SCALE).astype(BF16)
    cl = _rms(cl_ref[...], glat_ref[...])
    klat_ref[...] = cl
    clb = cl.astype(BF16)
    klatb_ref[...] = clb
    kf = _dot(clb, wuk_ref[...])
    hi, lo = _split_bf16(kf * kf)
    ss = _dot(hi, sel_ref[...]) + _dot(lo, sel_ref[...])
    krs_ref[...] = lax.rsqrt(ss[:, 0:MLA_HEADS] * (1.0 / D_NOPE) + EPS)
    sst = _dot_nt(selt_ref[...], hi) + _dot_nt(selt_ref[...], lo)
    krst_ref[...] = lax.rsqrt(sst * (1.0 / D_NOPE) + EPS)
    x = kr_ref[...]
    ms = jnp.sum(x * x, axis=-1, keepdims=True) * (1.0 / LANES)
    y = x * lax.rsqrt(ms + EPS) * gkr_ref[...]
    y = y * cos + _pair_swap(y, half) * sin
    krope_ref[...] = y[:, 0:D_ROPE]
    lane = lax.broadcasted_iota(jnp.int32, y.shape, 1)
    yb = y.astype(BF16)
    zero = jnp.zeros_like(yb)
    kropeb_ref[:, 0:LANES] = jnp.where(lane < D_ROPE, yb, zero)
    kropeb_ref[:, LANES:2 * LANES] = jnp.where(lane >= D_ROPE, yb, zero)


def mla_prep(proj, cos, sin, gq, gqr, glat, gkr, wukt, wuk, grp, sel, selt, tm):
    m = proj.shape[0]
    h = MLA_HEADS
    row = lambda i: (i, 0)
    c2 = lambda i: (0, 0)
    c3 = lambda i: (0, 0, 0)
    return pl.pallas_call(
        _mla_prep_body,
        grid=(m // tm,),
        in_specs=[
            pl.BlockSpec((tm, 1024), lambda i: (i, 1)),
            pl.BlockSpec((tm, 512), lambda i: (i, 4)),
            pl.BlockSpec((tm, 512), lambda i: (i, 5)),
            pl.BlockSpec((tm, LANES), lambda i: (i, 24)),
            pl.BlockSpec((tm, LANES), row),
            pl.BlockSpec((tm, LANES), row),
            pl.BlockSpec((1, D_NOPE), c2),
            pl.BlockSpec((1, 512), c2),
            pl.BlockSpec((1, D_LATENT), c2),
            pl.BlockSpec((1, LANES), c2),
            pl.BlockSpec((h, D_NOPE, D_LATENT), c3),
            pl.BlockSpec((D_LATENT, h * D_NOPE), c2),
            pl.BlockSpec((512, 512), c2),
            pl.BlockSpec((h * D_NOPE, LANES), c2),
            pl.BlockSpec((h, h * D_NOPE), c2),
        ],
        out_specs=[
            pl.BlockSpec((tm, h * D_LATENT), row),
            pl.BlockSpec((tm, 512), row),
            pl.BlockSpec((tm, D_LATENT), row),
            pl.BlockSpec((tm, D_LATENT), row),
            pl.BlockSpec((tm, D_ROPE), row),
            pl.BlockSpec((tm, 2 * LANES), row),
            pl.BlockSpec((tm, h), row),
            pl.BlockSpec((h, tm), lambda i: (0, i)),
        ],
        out_shape=[
            SDS((m, h * D_LATENT), BF16),
            SDS((m, 512), BF16),
            SDS((m, D_LATENT), F32),
            SDS((m, D_LATENT), BF16),
            SDS((m, D_ROPE), F32),
            SDS((m, 2 * LANES), BF16),
            SDS((m, h), F32),
            SDS((h, m), F32),
        ],
        compiler_params=_cparams(("parallel",)),
        name="mla_prep",
    )(proj, proj, proj, proj, cos, sin, gq, gqr, glat, gkr, wukt, wuk, grp, sel, selt)


def _attn_prompt_body(qi_ref, ki_ref, last_ref, ql_ref, qr_ref, kl_ref, kr_ref, rst_ref, wuv_ref,
                      o_ref, m_ref, l_ref, acc_ref, *, tq, tk):
    s_idx = pl.program_id(1)
    qi = qi_ref[s_idx]
    ki = ki_ref[s_idx]

    @pl.when(ki == 0)
    def _():
        m_ref[...] = jnp.full(m_ref.shape, NEG, F32)
        l_ref[...] = jnp.zeros(l_ref.shape, F32)
        acc_ref[...] = jnp.zeros(acc_ref.shape, F32)

    kl = kl_ref[...]
    rows = lax.broadcasted_iota(jnp.int32, (tq, tk), 0)
    cols = lax.broadcasted_iota(jnp.int32, (tq, tk), 1)
    keep = cols + (ki * tk - qi * tq) <= rows
    for h in range(MLA_HEADS):
        j, half = h % 4, h // 4
        s = _dot_nt(ql_ref[:, D_LATENT * h:D_LATENT * (h + 1)], kl) * rst_ref[h:h + 1, :]
        s = s + _dot_nt(qr_ref[:, LANES * j:LANES * (j + 1)], kr_ref[:, LANES * half:LANES * (half + 1)])
        s = jnp.where(keep, s, NEG)
        m_old = m_ref[h]
        m_new = jnp.maximum(m_old, jnp.max(s, axis=-1, keepdims=True))
        p = jnp.exp2(s - m_new)
        alpha = jnp.exp2(m_old - m_new)
        l_ref[h] = alpha * l_ref[h] + jnp.sum(p, axis=-1, keepdims=True)
        acc_ref[h] = alpha * acc_ref[h] + _dot(p.astype(BF16), kl)
        m_ref[h] = m_new

    @pl.when(last_ref[s_idx] == 1)
    def _():
        for h in range(MLA_HEADS):
            o_lat = acc_ref[h] / l_ref[h]
            o_ref[:, D_V * h:D_V * (h + 1)] = _dot(o_lat.astype(BF16), wuv_ref[h]).astype(o_ref.dtype)


def attn_prompt(qlat, qrope, klat, krope, krst, wuv, *, batch, seq, tq, tk):
    nq = seq // tq
    nk = seq // tk
    n_keys = [-(-((q + 1) * tq) // tk) for q in range(nq)]
    qi_tab = np.concatenate([np.full(n, q, np.int32) for q, n in enumerate(n_keys)])
    ki_tab = np.concatenate([np.arange(n, dtype=np.int32) for n in n_keys])
    last_tab = np.concatenate([(np.arange(n) == n - 1).astype(np.int32) for n in n_keys])
    n_tri = len(qi_tab)
    h = MLA_HEADS
    kern = functools.partial(_attn_prompt_body, tq=tq, tk=tk)
    grid_spec = pltpu.PrefetchScalarGridSpec(
        num_scalar_prefetch=3,
        grid=(batch, n_tri),
        in_specs=[
            pl.BlockSpec((tq, h * D_LATENT), lambda b, s, qi, ki, la: (b * nq + qi[s], 0)),
            pl.BlockSpec((tq, 512), lambda b, s, qi, ki, la: (b * nq + qi[s], 0)),
            pl.BlockSpec((tk, D_LATENT), lambda b, s, qi, ki, la: (b * nk + ki[s], 0)),
            pl.BlockSpec((tk, 2 * LANES), lambda b, s, qi, ki, la: (b * nk + ki[s], 0)),
            pl.BlockSpec((h, tk), lambda b, s, qi, ki, la: (0, b * nk + ki[s])),
            pl.BlockSpec((h, D_LATENT, D_V), lambda b, s, qi, ki, la: (0, 0, 0)),
        ],
        out_specs=pl.BlockSpec((tq, h * D_V), lambda b, s, qi, ki, la: (b * nq + qi[s], 0)),
        scratch_shapes=[
            pltpu.VMEM((h, tq, 1), F32),
            pltpu.VMEM((h, tq, 1), F32),
            pltpu.VMEM((h, tq, D_LATENT), F32),
        ],
    )
    return pl.pallas_call(
        kern,
        grid_spec=grid_spec,
        out_shape=SDS((batch * seq, h * D_V), BF16),
        compiler_params=_cparams(("parallel", "arbitrary")),
        name="attn_prompt",
    )(jnp.asarray(qi_tab), jnp.asarray(ki_tab), jnp.asarray(last_tab), qlat, qrope, klat, krope, krst, wuv)


def _attn_sample_body(pt_ref, ql_ref, qr_ref, *refs, n_pages_step, pages_group, dec_seq):
    p_n = n_pages_step
    lat_refs = refs[0:p_n]
    rope_refs = refs[p_n:2 * p_n]
    rs_refs = refs[2 * p_n:3 * p_n]
    klo_ref, kro_ref, rso_ref, o_ref, m_ref, l_ref, acc_ref = refs[3 * p_n:]
    j = pl.program_id(1)
    rows = dec_seq * MLA_HEADS

    @pl.when(j == 0)
    def _():
        m_ref[...] = jnp.full(m_ref.shape, NEG, F32)
        l_ref[...] = jnp.zeros(l_ref.shape, F32)
        acc_ref[...] = jnp.zeros(acc_ref.shape, F32)

    ql = ql_ref[...]
    qr = qr_ref[...]

    def update(s, kl):
        m_old = m_ref[...]
        m_new = jnp.maximum(m_old, jnp.max(s, axis=-1, keepdims=True))
        p = jnp.exp2(s - m_new)
        alpha = jnp.exp2(m_old - m_new)
        l_ref[...] = alpha * l_ref[...] + jnp.sum(p, axis=-1, keepdims=True)
        acc_ref[...] = alpha * acc_ref[...] + _dot(p.astype(BF16), kl)
        m_ref[...] = m_new

    def scores(kl, krt, rst, n):
        s = _dot_nt(ql, kl)
        s = (s.reshape(dec_seq, MLA_HEADS, n) * rst[None]).reshape(rows, n)
        return s + _dot(qr, krt)

    for g0 in range(0, p_n, pages_group):
        grp = slice(g0, g0 + pages_group)
        kl = jnp.concatenate([r[...].astype(BF16) for r in lat_refs[grp]], axis=0)
        krt = jnp.concatenate([r[...] for r in rope_refs[grp]], axis=1).astype(BF16)
        rst = jnp.concatenate([r[...] for r in rs_refs[grp]], axis=1)
        update(scores(kl, krt, rst, pages_group * PAGE), kl)

    @pl.when(j == pl.num_programs(1) - 1)
    def _():
        klo = klo_ref[...]
        s = scores(klo, kro_ref[...], rso_ref[...], PAGE)
        tok = lax.broadcasted_iota(jnp.int32, (rows, PAGE), 0) // MLA_HEADS
        key = lax.broadcasted_iota(jnp.int32, (rows, PAGE), 1)
        s = jnp.where(key <= tok, s, NEG)
        update(s, klo)
        o_ref[...] = (acc_ref[...] / l_ref[...]).astype(o_ref.dtype)


def attn_sample(page_table, qlat, qrope, cache_lat, cache_rope_t, cache_rs_t, klat_own, krope_own_t, rst_own,
                *, layer, n_pages_step):
    n_seq, n_pages = page_table.shape
    rows = qlat.shape[1]
    dec_seq = rows // MLA_HEADS
    p_n = n_pages_step
    n_steps = n_pages // p_n
    kern = functools.partial(_attn_sample_body, n_pages_step=p_n, pages_group=p_n, dec_seq=dec_seq)

    def page_spec(shape, i):
        return pl.BlockSpec((None, None) + shape,
                            lambda s, j, pt: (layer, pt[s * n_pages + j * p_n + i], 0, 0))

    own = lambda s, j, pt: (s, 0, 0)
    in_specs = [
        pl.BlockSpec((None, rows, D_LATENT), own),
        pl.BlockSpec((None, rows, D_ROPE), own),
    ]
    in_specs += [page_spec((PAGE, D_LATENT), i) for i in range(p_n)]
    in_specs += [page_spec((D_ROPE, PAGE), i) for i in range(p_n)]
    in_specs += [page_spec((MLA_HEADS, PAGE), i) for i in range(p_n)]
    in_specs += [
        pl.BlockSpec((None, PAGE, D_LATENT), own),
        pl.BlockSpec((None, D_ROPE, PAGE), own),
        pl.BlockSpec((None, MLA_HEADS, PAGE), own),
    ]
    grid_spec = pltpu.PrefetchScalarGridSpec(
        num_scalar_prefetch=1,
        grid=(n_seq, n_steps),
        in_specs=in_specs,
        out_specs=pl.BlockSpec((None, rows, D_LATENT), own),
        scratch_shapes=[
            pltpu.VMEM((rows, 1), F32),
            pltpu.VMEM((rows, 1), F32),
            pltpu.VMEM((rows, D_LATENT), F32),
        ],
    )
    args = [page_table.reshape(-1), qlat, qrope]
    args += [cache_lat] * p_n + [cache_rope_t] * p_n + [cache_rs_t] * p_n
    args += [klat_own, krope_own_t, rst_own]
    return pl.pallas_call(
        kern,
        grid_spec=grid_spec,
        out_shape=SDS((n_seq, rows, D_LATENT), BF16),
        compiler_params=_cparams(("parallel", "arbitrary")),
        name="attn_sample",
    )(*args)


def _head_matmul_body(x_ref, w_ref, o_ref):
    o_ref[...] = _dot(x_ref[...], w_ref[...]).astype(o_ref.dtype)


def head_matmul(x, w):
    m = x.shape[0]
    h, k, n = w.shape
    return pl.pallas_call(
        _head_matmul_body,
        grid=(h,),
        in_specs=[pl.BlockSpec((m, k), lambda i: (0, i)), pl.BlockSpec((None, k, n), lambda i: (i, 0, 0))],
        out_specs=pl.BlockSpec((m, n), lambda i: (0, i)),
        out_shape=SDS((m, h * n), BF16),
        compiler_params=_cparams(("parallel",)),
        name="head_matmul",
    )(x, w)


def _cumsum_rows(x, c):
    r = lax.broadcasted_iota(jnp.int32, (c, c), 0)
    s = lax.broadcasted_iota(jnp.int32, (c, c), 1)
    tri = (s <= r).astype(BF16)
    hi, lo = _split_bf16(x)
    return _dot(tri, hi) + _dot(tri, lo)


def _gla_body(q_ref, k_ref, v_ref, gg_ref, sm_ref, wa_ref, ba_ref, gn_ref, s0_ref,
              o_ref, so_ref, st_ref, b_ref, oi_ref, *, c, inner, carried):
    ci = pl.program_id(2)
    nb = c // SUBLANES
    sub = lax.broadcasted_iota(jnp.int32, (SUBLANES, 1), 0)

    if carried:
        @pl.when(ci == 0)
        def _():
            st_ref[...] = s0_ref[0].T

    def unit(u, _):
        r0 = pl.multiple_of(u * c, c)
        rows = pl.ds(r0, c)
        st = st_ref[...] if carried else s0_ref[u].T
        x = _dot(sm_ref[rows, :].astype(BF16), wa_ref[...]) + ba_ref[...]
        g = _log_sigmoid(x) * (1.0 / GLA_GATE_NORM)
        b = _cumsum_rows(g, c)
        b_ref[...] = b
        q = q_ref[rows, :]
        k = k_ref[rows, :]
        v = v_ref[rows, :]
        o_inter = _dot_nt((q * jnp.exp(b)).astype(BF16), st.astype(BF16))

        for bi in range(nb):
            accs = [jnp.zeros((SUBLANES, GLA_DV), F32)] * SUBLANES
            for bj in range(bi + 1):
                kj = k_ref[pl.ds(r0 + SUBLANES * bj, SUBLANES), :]
                vj = v_ref[pl.ds(r0 + SUBLANES * bj, SUBLANES), :]
                bj_rows = b_ref[SUBLANES * bj:SUBLANES * (bj + 1), :]
                for tt in range(SUBLANES):
                    t = SUBLANES * bi + tt
                    qt = q_ref[pl.ds(r0 + t, 1), :]
                    dlt = b_ref[t:t + 1, :] - bj_rows
                    if bj == bi:
                        dlt = jnp.where(sub <= tt, dlt, NEG)
                    a = jnp.sum(jnp.exp(dlt) * kj * qt, axis=-1, keepdims=True)
                    accs[tt] = accs[tt] + a * vj
            for tt in range(SUBLANES):
                t = SUBLANES * bi + tt
                oi_ref[t:t + 1, :] = jnp.sum(accs[tt], axis=0, keepdims=True)

        o = (o_inter + oi_ref[...]) * (GLA_DK ** -0.5)
        o = _rms(o, gn_ref[...])
        gg = gg_ref[rows, :]
        o_ref[rows, :] = (o * (gg * jax.nn.sigmoid(gg))).astype(o_ref.dtype)

        b_last = b[c - 1:c, :]
        kd = k * jnp.exp(b_last - b)
        st_new = st * jnp.exp(b_last) + _dot_tn(v.astype(BF16), kd.astype(BF16))
        if carried:
            st_ref[...] = st_new
        else:
            so_ref[u] = st_new.T
        return 0

    lax.fori_loop(0, inner, unit, 0)

    if carried:
        @pl.when(ci == pl.num_programs(2) - 1)
        def _():
            so_ref[0] = st_ref[...].T


def gla_mix(proj, wa, ba, gn, s0, *, n_seq, n_chunks, c):
    m = proj.shape[0]
    hh = GLA_HEADS
    carried = n_chunks > 1
    if carried:
        inner = _pick(n_chunks, (4, 2, 1))
        seqs, steps = 1, n_chunks // inner
    else:
        inner = _pick(n_seq, (16, 8, 4, 2, 1))
        seqs, steps = inner, 1
    rows = inner * c
    kern = functools.partial(_gla_body, c=c, inner=inner, carried=carried)
    row = lambda g, h, ci: g * steps + ci
    st4 = lambda g, h, ci: (g, h, 0, 0)
    return pl.pallas_call(
        kern,
        grid=(n_seq // seqs, hh, steps),
        in_specs=[
            pl.BlockSpec((rows, GLA_DK), lambda g, h, ci: (row(g, h, ci), h)),
            pl.BlockSpec((rows, GLA_DK), lambda g, h, ci: (row(g, h, ci), 4 + h)),
            pl.BlockSpec((rows, GLA_DV), lambda g, h, ci: (row(g, h, ci), 4 + h)),
            pl.BlockSpec((rows, GLA_DV), lambda g, h, ci: (row(g, h, ci), 8 + h)),
            pl.BlockSpec((rows, LANES), lambda g, h, ci: (row(g, h, ci), 48)),
            pl.BlockSpec((LANES, GLA_DK), lambda g, h, ci: (0, h)),
            pl.BlockSpec((1, GLA_DK), lambda g, h, ci: (0, h)),
            pl.BlockSpec((1, GLA_DV), lambda g, h, ci: (0, 0)),
            pl.BlockSpec((seqs, None, GLA_DK, GLA_DV), st4),
        ],
        out_specs=[
            pl.BlockSpec((rows, GLA_DV), lambda g, h, ci: (row(g, h, ci), h)),
            pl.BlockSpec((seqs, None, GLA_DK, GLA_DV), st4),
        ],
        out_shape=[SDS((m, hh * GLA_DV), BF16), SDS((n_seq, hh, GLA_DK, GLA_DV), F32)],
        scratch_shapes=[
            pltpu.VMEM((GLA_DV, GLA_DK), F32),
            pltpu.VMEM((c, GLA_DK), F32),
            pltpu.VMEM((c, GLA_DV), F32),
        ],
        compiler_params=_cparams(("parallel", "parallel", "arbitrary")),
        name="gla_mix",
    )(proj, proj, proj, proj, proj, wa, ba, gn, s0)


ML_I_LANE = GLA_RANK
ML_F_LANE = GLA_RANK + ML_HEADS


def _mlstm_body(q_ref, k_ref, v_ref, mo_ref, sm_ref, bias_ref, gn_ref, c0_ref, n0_ref, m0_ref,
                o_ref, co_ref, no_ref, mo_out_ref, cs_ref, ns_ref, ms_ref, tr_ref, *, c, inner, carried):
    ci = pl.program_id(1)
    r = lax.broadcasted_iota(jnp.int32, (c, c), 0)
    s = lax.broadcasted_iota(jnp.int32, (c, c), 1)
    causal = s <= r
    lane_h = lax.broadcasted_iota(jnp.int32, (1, ML_HEADS), 1)

    if carried:
        @pl.when(ci == 0)
        def _():
            cs_ref[...] = c0_ref[0]
            ns_ref[...] = n0_ref[0]
            ms_ref[...] = m0_ref[0]

    def unit(u, _):
        r0 = pl.multiple_of(u * c, c)
        rows = pl.ds(r0, c)
        pre = sm_ref[rows, :] + bias_ref[...]
        bcum = _cumsum_rows(_log_sigmoid(pre), c)
        tr_ref[0] = pre.T
        tr_ref[1] = bcum.T
        m_all = ms_ref[...] if carried else m0_ref[u]
        m_row = jnp.zeros((1, ML_HEADS), F32)
        for h in range(ML_HEADS):
            li, lf = ML_I_LANE + h, ML_F_LANE + h
            i_col = pre[:, li:li + 1]
            b_col = bcum[:, lf:lf + 1]
            i_row = tr_ref[0, li:li + 1, :]
            b_row = tr_ref[1, lf:lf + 1, :]
            m_prev = m_all[:, h:h + 1]
            rel = jnp.where(causal, b_col - b_row + i_row, NEG)
            carry_log = b_col + m_prev
            m_t = jnp.maximum(carry_log, jnp.max(rel, axis=-1, keepdims=True))
            w_intra = jnp.exp(rel - m_t)
            w_carry = jnp.exp(carry_log - m_t)

            q = q_ref[rows, ML_DK * h:ML_DK * (h + 1)] * (ML_DK ** -0.5)
            k = k_ref[rows, ML_DK * h:ML_DK * (h + 1)]
            v = v_ref[rows, ML_DV * h:ML_DV * (h + 1)]
            qb = q.astype(BF16)
            vb = v.astype(BF16)
            cm = cs_ref[h] if carried else c0_ref[u, h]
            nv = ns_ref[h:h + 1, :] if carried else n0_ref[u, h:h + 1, :]
            qk = _dot_nt(qb, k.astype(BF16)) * w_intra
            num = _dot(qk.astype(BF16), vb) + w_carry * _dot(qb, cm.astype(BF16))
            den = jnp.sum(qk, axis=-1, keepdims=True) + w_carry * jnp.sum(q * nv, axis=-1, keepdims=True)
            hh = num / jnp.maximum(jnp.abs(den), jnp.exp(-m_t))

            mo = mo_ref[rows, ML_DV * h:ML_DV * (h + 1)]
            o_ref[rows, ML_DV * h:ML_DV * (h + 1)] = (jax.nn.sigmoid(mo) * _rms(hh, gn_ref[...])).astype(o_ref.dtype)

            m_new = m_t[c - 1:c, :]
            b_last = b_col[c - 1:c, :]
            w_state = jnp.exp(b_last - b_col + i_col - m_new)
            decay = jnp.exp(b_last + m_prev - m_new)
            kw = k * w_state
            cm_new = decay * cm + _dot_tn(kw.astype(BF16), vb)
            nv_new = decay * nv + jnp.sum(kw, axis=0, keepdims=True)
            if carried:
                cs_ref[h] = cm_new
                ns_ref[h:h + 1, :] = nv_new
            else:
                co_ref[u, h] = cm_new
                no_ref[u, h:h + 1, :] = nv_new
            m_row = jnp.where(lane_h == h, m_new, m_row)
        if carried:
            ms_ref[...] = m_row
        else:
            mo_out_ref[u] = m_row
        return 0

    lax.fori_loop(0, inner, unit, 0)

    if carried:
        @pl.when(ci == pl.num_programs(1) - 1)
        def _():
            co_ref[0] = cs_ref[...]
            no_ref[0] = ns_ref[...]
            mo_out_ref[0] = ms_ref[...]


def mlstm_mix(proj, bias, gn, c0, n0, m0, *, n_seq, n_chunks, c):
    m = proj.shape[0]
    hh = ML_HEADS
    carried = n_chunks > 1
    if carried:
        inner = _pick(n_chunks, (4, 2, 1))
        seqs, steps = 1, n_chunks // inner
    else:
        inner = _pick(n_seq, (8, 4, 2, 1))
        seqs, steps = inner, 1
    rows = inner * c
    kern = functools.partial(_mlstm_body, c=c, inner=inner, carried=carried)
    row = lambda g, ci: g * steps + ci
    st4 = lambda g, ci: (g, 0, 0, 0)
    st3 = lambda g, ci: (g, 0, 0)
    return pl.pallas_call(
        kern,
        grid=(n_seq // seqs, steps),
        in_specs=[
            pl.BlockSpec((rows, hh * ML_DK), lambda g, ci: (row(g, ci), 6)),
            pl.BlockSpec((rows, hh * ML_DK), lambda g, ci: (row(g, ci), 7)),
            pl.BlockSpec((rows, hh * ML_DV), lambda g, ci: (row(g, ci), 4)),
            pl.BlockSpec((rows, hh * ML_DV), lambda g, ci: (row(g, ci), 5)),
            pl.BlockSpec((rows, LANES), lambda g, ci: (row(g, ci), 48)),
            pl.BlockSpec((1, LANES), lambda g, ci: (0, 0)),
            pl.BlockSpec((1, ML_DV), lambda g, ci: (0, 0)),
            pl.BlockSpec((seqs, hh, ML_DK, ML_DV), st4),
            pl.BlockSpec((seqs, hh, ML_DK), st3),
            pl.BlockSpec((seqs, 1, hh), st3),
        ],
        out_specs=[
            pl.BlockSpec((rows, hh * ML_DV), lambda g, ci: (row(g, ci), 0)),
            pl.BlockSpec((seqs, hh, ML_DK, ML_DV), st4),
            pl.BlockSpec((seqs, hh, ML_DK), st3),
            pl.BlockSpec((seqs, 1, hh), st3),
        ],
        out_shape=[
            SDS((m, hh * ML_DV), BF16),
            SDS((n_seq, hh, ML_DK, ML_DV), F32),
            SDS((n_seq, hh, ML_DK), F32),
            SDS((n_seq, 1, hh), F32),
        ],
        scratch_shapes=[
            pltpu.VMEM((hh, ML_DK, ML_DV), F32),
            pltpu.VMEM((hh, ML_DK), F32),
            pltpu.VMEM((1, hh), F32),
            pltpu.VMEM((2, LANES, c), F32),
        ],
        compiler_params=_cparams(("parallel", "arbitrary")),
        name="mlstm_mix",
    )(proj, proj, proj, proj, proj, bias, gn, c0, n0, m0)


def _pad_cols(w, n):
    return jnp.pad(w, ((0, 0), (0, n - w.shape[1])))


def _even_in_weight(w):
    d = w.shape[0]
    hq = MLA_HEADS * (D_NOPE + D_ROPE)
    u = w[:, :1024]
    q = w[:, 1024:1024 + hq].reshape(d, MLA_HEADS, D_NOPE + D_ROPE)
    q_nope = q[:, :, :D_NOPE].reshape(d, MLA_HEADS * D_NOPE)
    q_rope = q[:, :, D_NOPE:].reshape(d, 2, 4, D_ROPE).transpose(0, 2, 1, 3).reshape(d, MLA_HEADS * D_ROPE)
    c_lat = w[:, 1024 + hq:1024 + hq + D_LATENT]
    k_r = w[:, 1024 + hq + D_LATENT:]
    return _pad_cols(jnp.concatenate([u, q_nope, q_rope, c_lat, k_r, k_r], axis=1), 3328).astype(BF16)


def _odd_in_weight(w):
    sizes = (512, 512, 1024, GLA_RANK, 1024, 512, 512, 1024, ML_HEADS, ML_HEADS, 1024)
    offs = np.concatenate([[0], np.cumsum(sizes)])
    seg = [w[:, offs[i]:offs[i + 1]] for i in range(len(sizes))]
    gq, gk, gv, ga, gg, mq, mk, mv, mi, mf, mo = seg
    small = _pad_cols(jnp.concatenate([ga, mi, mf], axis=1), LANES)
    return _pad_cols(jnp.concatenate([gq, gk, gv, gg, mq, mk, mv, mo, small], axis=1), 6400).astype(BF16)


def _s5_params(a_re, a_im, log_dt, b_re, b_im, c_re, c_im):
    g, p = a_re.shape
    ar = jnp.minimum(a_re, -1e-4)
    ai = a_im
    dt = jnp.exp(log_dt)[:, None]
    mag = jnp.exp(ar * dt)
    lr = mag * jnp.cos(ai * dt)
    li = mag * jnp.sin(ai * dt)
    den = ar * ar + ai * ai
    cr = ((lr - 1.0) * ar + li * ai) / den
    ci = (li * ar - (lr - 1.0) * ai) / den
    bb_re = cr[..., None] * b_re - ci[..., None] * b_im
    bb_im = cr[..., None] * b_im + ci[..., None] * b_re
    eye = jnp.eye(8, dtype=F32)

    def in_slab(bb):
        x = bb.transpose(0, 2, 1).reshape(S5_SLABS, 8, S5_GROUP, p)
        return jnp.einsum("jaip,ab->jaibp", x, eye).reshape(S5_SLABS, 8 * S5_GROUP, 8 * p)

    def out_slab(cc):
        x = cc.transpose(0, 2, 1).reshape(S5_SLABS, 8, p, S5_GROUP)
        return jnp.einsum("japi,ab->japbi", x, eye).reshape(S5_SLABS, 8 * p, 8 * S5_GROUP)

    bw = jnp.concatenate([in_slab(bb_re), in_slab(bb_im)], axis=2).astype(BF16)
    cw = jnp.concatenate([out_slab(c_re), -out_slab(c_im)], axis=1).astype(BF16)
    lam = jnp.stack([lr.reshape(S5_ROWS, LANES), li.reshape(S5_ROWS, LANES)])
    return lam, bw, cw


def _rope_tables(pos):
    half = D_ROPE // 2
    inv_freq = ROPE_THETA ** (-jnp.arange(half, dtype=F32) / half)
    ang = pos.astype(F32)[:, None] * inv_freq
    cos, sin = jnp.cos(ang), jnp.sin(ang)
    return jnp.tile(cos, (1, 4)), jnp.tile(jnp.concatenate([-sin, sin], axis=1), (1, 2))


def _pick(n, prefs):
    for t in prefs:
        if n % t == 0:
            return t
    return n


def _even_layer(x, *, n_seq, seq_len, pos, h0_re, h0_im, attend, p):
    m = x.shape[0]
    tm = _pick(m, (512, 256))
    proj = norm_matmul(x, p["norm_mix"], p["w_in"], tm, 1664)
    if seq_len >= 256:
        tile = _pick(seq_len, (256,))
        s5_kw = dict(n_groups=n_seq, tiles_per_group=seq_len // tile, tm=tile, n_seq=1, seq_len=tile)
    else:
        per = max(1, 256 // seq_len)
        per = _pick(n_seq, (per, 16, 8, 4, 2, 1))
        s5_kw = dict(n_groups=n_seq // per, tiles_per_group=1, tm=per * seq_len, n_seq=per, seq_len=seq_len)
    h0 = jnp.stack([h0_re.reshape(n_seq, S5_ROWS, LANES), h0_im.reshape(n_seq, S5_ROWS, LANES)], axis=1)
    s5_y, h_last = s5_mix(proj, h0, p["lam"], p["bw"], p["cw"], p["s5_d"], p["w_glu"], p["b_glu"], **s5_kw)
    s5_re = h_last[:, 0].reshape(n_seq, S5_ROWS * LANES // S5_STATE, S5_STATE)
    s5_im = h_last[:, 1].reshape(n_seq, S5_ROWS * LANES // S5_STATE, S5_STATE)
    cos, sin = _rope_tables(pos)
    qlat, qrope, klat, klat_b, krope, krope_b, krs, krst = mla_prep(
        proj, cos, sin, p["g_q"], p["g_qr"], p["g_lat"], p["g_kr"], p["wukt"], p["wuk"], p["grp"], p["sel"],
        p["selt"], _pick(m, (256,)))
    o = attend(qlat, qrope, klat_b, krope_b, krst)
    x = out_proj(x, s5_y, o, p["w_out_a"], p["w_out_b"], tm)
    return x, (klat, krope, krs, s5_re, s5_im)


def _odd_layer(x, *, n_seq, seq_len, s0, c0, n0, m0, p):
    m = x.shape[0]
    tm = _pick(m, (512, 256))
    proj = norm_matmul(x, p["norm_mix"], p["w_in"], tm, 1280)
    c = math.gcd(seq_len, CHUNK)
    n_chunks = seq_len // c
    o_gla, s_gla = gla_mix(proj, p["wa"], p["ba"], p["g_gla"], s0, n_seq=n_seq, n_chunks=n_chunks, c=c)
    h_ml, cm, nv, mm = mlstm_mix(proj, p["ml_bias"], p["g_ml"], c0, n0, m0[:, None, :],
                                 n_seq=n_seq, n_chunks=n_chunks, c=c)
    x = out_proj(x, o_gla, h_ml, p["w_out_a"], p["w_out_b"], tm)
    return x, (s_gla, cm, nv, mm[:, 0, :])


def kernel(x_prompt, x_sample, cache_mla_latent, cache_mla_k_rope, cache_mla_k_rscale, state_s5_re, state_s5_im, state_gla, state_mlstm_c, state_mlstm_n, state_mlstm_m, page_table, norm_mix, norm_ffn, ffn_w1, ffn_w3, ffn_w2, e_w_in, s5_a_re, s5_a_im, s5_log_dt, s5_b_re, s5_b_im, s5_c_re, s5_c_im, s5_d, s5_w_glu, s5_b_glu, mla_g_qnope, mla_g_qrope, mla_g_knope, mla_g_krope, mla_g_latent, mla_w_uk, mla_w_uv, e_w_out, o_w_in, gla_w_a2, gla_b_a, gla_g_norm, ml_b_i, ml_b_f, ml_g_norm, o_w_out):
    bp, lp, d = x_prompt.shape
    bs, ls, _ = x_sample.shape
    depth = norm_mix.shape[0]
    past_len = page_table.shape[1] * PAGE
    pos_p = jnp.arange(lp, dtype=jnp.int32)
    pos_s = past_len + jnp.arange(ls, dtype=jnp.int32)
    xp = x_prompt.reshape(bp * lp, d)
    xs = x_sample.reshape(bs * ls, d)
    even_p, even_s, odd_p, odd_s = [], [], [], []
    for l in range(depth):
        j = l // 2
        if l % 2 == 0:
            lam, bw, cw = _s5_params(s5_a_re[j], s5_a_im[j], s5_log_dt[j], s5_b_re[j], s5_b_im[j],
                                     s5_c_re[j], s5_c_im[j])
            s5w = s5_a_re.shape[1] * S5_GROUP
            h = MLA_HEADS
            wuk = mla_w_uk[j]
            sel = (jnp.arange(h * D_NOPE)[:, None] // D_NOPE == jnp.arange(LANES)[None, :]).astype(BF16)
            gi = jnp.arange(512) // D_ROPE
            p = dict(
                norm_mix=norm_mix[l][None], w_in=_even_in_weight(e_w_in[j]),
                lam=lam, bw=bw, cw=cw, s5_d=s5_d[j][None], w_glu=s5_w_glu[j].astype(BF16), b_glu=s5_b_glu[j][None],
                g_q=(mla_g_qnope[j] * mla_g_knope[j])[None], g_qr=jnp.tile(mla_g_qrope[j], h)[None],
                g_lat=mla_g_latent[j][None], g_kr=jnp.tile(mla_g_krope[j], 2)[None],
                wukt=wuk.transpose(1, 2, 0).astype(BF16), wuk=wuk.reshape(D_LATENT, h * D_NOPE).astype(BF16),
                grp=(gi[:, None] == gi[None, :]).astype(BF16), sel=sel, selt=sel[:, :h].T,
                w_out_a=e_w_out[j][:s5w].astype(BF16), w_out_b=e_w_out[j][s5w:].astype(BF16),
            )
            wuv = mla_w_uv[j].transpose(1, 0, 2).astype(BF16)

            def attend_p(qlat, qrope, klat_b, krope_b, krst):
                return attn_prompt(qlat, qrope, klat_b, krope_b, krst, wuv, batch=bp, seq=lp,
                                   tq=_pick(lp, (256,)), tk=_pick(lp, (512, 256)))

            def attend_s(qlat, qrope, klat_b, krope_b, krst):
                rows = ls * h
                qr = qrope.reshape(bs * ls, 4, 2, D_ROPE).transpose(0, 2, 1, 3).reshape(bs, rows, D_ROPE)
                klat_own = jnp.pad(klat_b.reshape(bs, ls, D_LATENT), ((0, 0), (0, PAGE - ls), (0, 0)))
                lane_pad = ((0, 0), (0, 0), (0, PAGE - ls))
                krope_own_t = jnp.pad(krope_b[:, :D_ROPE].reshape(bs, ls, D_ROPE).transpose(0, 2, 1), lane_pad)
                rst_own = jnp.pad(krst.reshape(h, bs, ls).transpose(1, 0, 2), lane_pad)
                n_pages = page_table.shape[1]
                o_lat = attn_sample(page_table, qlat.reshape(bs, rows, D_LATENT), qr, cache_mla_latent,
                                    jnp.swapaxes(cache_mla_k_rope, 2, 3), jnp.swapaxes(cache_mla_k_rscale, 2, 3),
                                    klat_own, krope_own_t, rst_own,
                                    layer=j, n_pages_step=_pick(n_pages, (32, 16, 8, 4, 2, 1)))
                return head_matmul(o_lat.reshape(bs * ls, h * D_LATENT), wuv)

            zeros = jnp.zeros((bp, s5_a_re.shape[1], S5_STATE), F32)
            xp, st_p = _even_layer(xp, n_seq=bp, seq_len=lp, pos=jnp.tile(pos_p, bp), h0_re=zeros, h0_im=zeros,
                                   attend=attend_p, p=p)
            xs, st_s = _even_layer(xs, n_seq=bs, seq_len=ls, pos=jnp.tile(pos_s, bs), h0_re=state_s5_re[j],
                                   h0_im=state_s5_im[j], attend=attend_s, p=p)
            even_p.append((st_p[0].reshape(bp, lp, -1), st_p[1].reshape(bp, lp, -1), st_p[2].reshape(bp, lp, -1),
                           st_p[3], st_p[4]))
            even_s.append((st_s[0].reshape(bs, ls, -1), st_s[1].reshape(bs, ls, -1), st_s[2].reshape(bs, ls, -1),
                           st_s[3], st_s[4]))
        else:
            hk = GLA_HEADS * GLA_DK
            wo = o_w_out[j]
            bias = jnp.zeros((LANES,), F32)
            bias = bias.at[ML_I_LANE:ML_I_LANE + ML_HEADS].set(ml_b_i[j]).at[ML_F_LANE:ML_F_LANE + ML_HEADS].set(ml_b_f[j])
            p = dict(
                norm_mix=norm_mix[l][None], w_in=_odd_in_weight(o_w_in[j]),
                wa=jnp.pad(gla_w_a2[j], ((0, LANES - GLA_RANK), (0, 0))).astype(BF16), ba=gla_b_a[j][None],
                g_gla=gla_g_norm[j][None], ml_bias=bias[None], g_ml=ml_g_norm[j][None],
                w_out_a=wo[:GLA_HEADS * GLA_DV].astype(BF16), w_out_b=wo[GLA_HEADS * GLA_DV:].astype(BF16),
            )
            del hk
            xp, st_p = _odd_layer(xp, n_seq=bp, seq_len=lp, s0=jnp.zeros((bp, GLA_HEADS, GLA_DK, GLA_DV), F32),
                                  c0=jnp.zeros((bp, ML_HEADS, ML_DK, ML_DV), F32),
                                  n0=jnp.zeros((bp, ML_HEADS, ML_DK), F32), m0=jnp.zeros((bp, ML_HEADS), F32), p=p)
            xs, st_s = _odd_layer(xs, n_seq=bs, seq_len=ls, s0=state_gla[j], c0=state_mlstm_c[j],
                                  n0=state_mlstm_n[j], m0=state_mlstm_m[j], p=p)
            odd_p.append(st_p)
            odd_s.append(st_s)
        w1 = ffn_w1[l].astype(BF16)
        w3 = ffn_w3[l].astype(BF16)
        w2 = ffn_w2[l].astype(BF16)
        g = norm_ffn[l][None]
        tf = _pick(w1.shape[1], (512, 256, 128))
        xp = ffn(xp, g, w1, w3, w2, _pick(xp.shape[0], (1024, 512, 256)), tf)
        xs = ffn(xs, g, w1, w3, w2, _pick(xs.shape[0], (1024, 512, 256)), tf)
    p_lat, p_rope, p_rscale, p_s5_re, p_s5_im = [jnp.stack(t) for t in zip(*even_p)]
    s_lat, s_rope, s_rscale, s_s5_re, s_s5_im = [jnp.stack(t) for t in zip(*even_s)]
    p_gla, p_ml_c, p_ml_n, p_ml_m = [jnp.stack(t) for t in zip(*odd_p)]
    s_gla, s_ml_c, s_ml_n, s_ml_m = [jnp.stack(t) for t in zip(*odd_s)]
    return (xp.reshape(bp, lp, d), xs.reshape(bs, ls, d),
            p_lat, p_rope, p_rscale, p_s5_re, p_s5_im, p_gla, p_ml_c, p_ml_n, p_ml_m,
            s_lat, s_rope, s_rscale, s_s5_re, s_s5_im, s_gla, s_ml_c, s_ml_n, s_ml_m)
```

```python
import functools
import math

import jax
import jax.numpy as jnp
import numpy as np
from jax import lax
from jax.experimental import pallas as pl
from jax.experimental.pallas import tpu as pltpu

F32, BF16 = jnp.float32, jnp.bfloat16
SDS = jax.ShapeDtypeStruct

EPS = 1e-6
LANES = 128
SUBLANES = 8
VMEM_LIMIT = 52 * 1024 * 1024

S5_GROUP = 16
S5_STATE = 64
MLA_HEADS = 8
D_NOPE = 128
D_ROPE = 64
D_V = 128
D_LATENT = 512
ROPE_THETA = 10000.0
MLA_SCALE = (D_NOPE + D_ROPE) ** -0.5
Q_SCALE = MLA_SCALE * math.log2(math.e)
PAGE = 128
GLA_HEADS = 4
GLA_DK = 128
GLA_DV = 256
GLA_RANK = 16
GLA_GATE_NORM = 16.0
ML_HEADS = 4
ML_DK = 128
ML_DV = 256
CHUNK = 64
NEG = -1e30


def _cparams(sem):
    return pltpu.CompilerParams(dimension_semantics=sem, vmem_limit_bytes=VMEM_LIMIT)


def _rms(x, g):
    ms = jnp.mean(x * x, axis=-1, keepdims=True)
    return x * lax.rsqrt(ms + EPS) * g


def _dot(a, b):
    return jnp.dot(a, b, preferred_element_type=F32)


def _dot_nt(a, b):
    return lax.dot_general(a, b, (((1,), (1,)), ((), ())), preferred_element_type=F32)


def _dot_tn(a, b):
    return lax.dot_general(a, b, (((0,), (0,)), ((), ())), preferred_element_type=F32)


def _split_bf16(x):
    hi = x.astype(BF16)
    lo = (x - hi.astype(F32)).astype(BF16)
    return hi, lo


def _log_sigmoid(x):
    return jnp.minimum(x, 0.0) - jnp.log(1.0 + jnp.exp(-jnp.abs(x)))


def _norm_matmul_body(x_ref, g_ref, w_ref, o_ref, xn_ref):
    @pl.when(pl.program_id(1) == 0)
    def _():
        xn_ref[...] = _rms(x_ref[...], g_ref[...]).astype(BF16)

    o_ref[...] = _dot(xn_ref[...], w_ref[...])


def norm_matmul(x, g, w, tm, tn):
    m, k = x.shape
    n = w.shape[1]
    return pl.pallas_call(
        _norm_matmul_body,
        grid=(m // tm, n // tn),
        in_specs=[
            pl.BlockSpec((tm, k), lambda i, j: (i, 0)),
            pl.BlockSpec((1, k), lambda i, j: (0, 0)),
            pl.BlockSpec((k, tn), lambda i, j: (0, j)),
        ],
        out_specs=pl.BlockSpec((tm, tn), lambda i, j: (i, j)),
        out_shape=SDS((m, n), F32),
        scratch_shapes=[pltpu.VMEM((tm, k), BF16)],
        compiler_params=_cparams(("parallel", "arbitrary")),
        name="norm_matmul",
    )(x, g, w)


def _ffn_body(x_ref, g_ref, w1_ref, w3_ref, w2_ref, o_ref, xn_ref):
    @pl.when(pl.program_id(1) == 0)
    def _():
        x = x_ref[...]
        xn_ref[...] = _rms(x, g_ref[...]).astype(BF16)
        o_ref[...] = x

    xn = xn_ref[...]
    a = _dot(xn, w1_ref[...])
    b = _dot(xn, w3_ref[...])
    h = (a * jax.nn.sigmoid(a) * b).astype(BF16)
    o_ref[...] += _dot(h, w2_ref[...])


def ffn(x, g, w1, w3, w2, layer, tm, tf):
    m, d = x.shape
    f = w1.shape[2]
    return pl.pallas_call(
        _ffn_body,
        grid=(m // tm, f // tf),
        in_specs=[
            pl.BlockSpec((tm, d), lambda i, j: (i, 0), pipeline_mode=pl.Buffered(1)),
            pl.BlockSpec((1, d), lambda i, j: (0, 0)),
            pl.BlockSpec((None, d, tf), lambda i, j: (layer, 0, j)),
            pl.BlockSpec((None, d, tf), lambda i, j: (layer, 0, j)),
            pl.BlockSpec((None, tf, d), lambda i, j: (layer, j, 0)),
        ],
        out_specs=pl.BlockSpec((tm, d), lambda i, j: (i, 0)),
        out_shape=SDS((m, d), F32),
        scratch_shapes=[pltpu.VMEM((tm, d), BF16)],
        compiler_params=_cparams(("parallel", "arbitrary")),
        name="ffn",
    )(x, g, w1, w3, w2)


def _out_proj_body(x_ref, a_ref, b_ref, wa_ref, wb_ref, o_ref):
    o_ref[...] = x_ref[...] + _dot(a_ref[...], wa_ref[...]) + _dot(b_ref[...], wb_ref[...])


def out_proj(x, a, b, wa, wb, tm):
    m, d = x.shape
    ka, kb = a.shape[1], b.shape[1]
    return pl.pallas_call(
        _out_proj_body,
        grid=(m // tm,),
        in_specs=[
            pl.BlockSpec((tm, d), lambda i: (i, 0)),
            pl.BlockSpec((tm, ka), lambda i: (i, 0)),
            pl.BlockSpec((tm, kb), lambda i: (i, 0)),
            pl.BlockSpec((ka, d), lambda i: (0, 0)),
            pl.BlockSpec((kb, d), lambda i: (0, 0)),
        ],
        out_specs=pl.BlockSpec((tm, d), lambda i: (i, 0)),
        out_shape=SDS((m, d), F32),
        compiler_params=_cparams(("parallel",)),
        name="out_proj",
    )(x, a, b, wa, wb)


S5_SLABS = 8
S5_ROWS = 32


def _s5_body(u_ref, h0_ref, lam_ref, bw_ref, cw_ref, d_ref, wg_ref, bg_ref,
             o_ref, hl_ref, bu_ref, hs_ref, carry_ref, *, tm, n_seq, seq_len):
    first_tile = pl.program_id(1) == 0
    u = u_ref[...]
    ub = u.astype(BF16)
    nr = 2 * S5_ROWS
    for j in range(S5_SLABS):
        res = _dot(ub[:, LANES * j:LANES * (j + 1)], bw_ref[j])
        for q in range(4):
            bu_ref[pl.ds(4 * j + q, tm, stride=nr), :] = res[:, LANES * q:LANES * (q + 1)]
            bu_ref[pl.ds(S5_ROWS + 4 * j + q, tm, stride=nr), :] = res[:, 512 + LANES * q:512 + LANES * (q + 1)]

    lr = lam_ref[0]
    li = lam_ref[1]

    def seq_body(s, _):
        hr = jnp.where(first_tile, h0_ref[s, 0], carry_ref[0])
        hi = jnp.where(first_tile, h0_ref[s, 1], carry_ref[1])

        def step(t, c):
            hr, hi = c
            row = pl.multiple_of((s * seq_len + t) * nr, nr)
            br = bu_ref[pl.ds(row, S5_ROWS), :]
            bi = bu_ref[pl.ds(row + S5_ROWS, S5_ROWS), :]
            nhr = lr * hr - li * hi + br
            nhi = lr * hi + li * hr + bi
            hs_ref[pl.ds(row, S5_ROWS), :] = nhr
            hs_ref[pl.ds(row + S5_ROWS, S5_ROWS), :] = nhi
            return nhr, nhi

        hr, hi = lax.fori_loop(0, seq_len, step, (hr, hi), unroll=min(seq_len, 8))
        carry_ref[0] = hr
        carry_ref[1] = hi
        hl_ref[s, 0] = hr
        hl_ref[s, 1] = hi
        return 0

    lax.fori_loop(0, n_seq, seq_body, 0)

    ys = []
    for j in range(S5_SLABS):
        parts = [hs_ref[pl.ds(4 * j + q, tm, stride=nr), :] for q in range(4)]
        parts += [hs_ref[pl.ds(S5_ROWS + 4 * j + q, tm, stride=nr), :] for q in range(4)]
        lhs = jnp.concatenate(parts, axis=1).astype(BF16)
        ys.append(_dot(lhs, cw_ref[j]))
    y = jnp.concatenate(ys, axis=1) + d_ref[...] * u
    z = jax.nn.gelu(y)
    gate = jax.nn.sigmoid(_dot(z.astype(BF16), wg_ref[...]) + bg_ref[...])
    o_ref[...] = (z * gate).astype(o_ref.dtype)


def s5_mix(proj, h0, lam, bw, cw, d, wg, bg, *, n_groups, tiles_per_group, tm, n_seq, seq_len):
    width = S5_SLABS * LANES
    m = proj.shape[0]
    kern = functools.partial(_s5_body, tm=tm, n_seq=n_seq, seq_len=seq_len)
    const3 = lambda g, t: (0, 0, 0)
    return pl.pallas_call(
        kern,
        grid=(n_groups, tiles_per_group),
        in_specs=[
            pl.BlockSpec((tm, width), lambda g, t: (g * tiles_per_group + t, 0)),
            pl.BlockSpec((n_seq, 2, S5_ROWS, LANES), lambda g, t: (g, 0, 0, 0)),
            pl.BlockSpec((2, S5_ROWS, LANES), const3),
            pl.BlockSpec((S5_SLABS, LANES, 1024), const3),
            pl.BlockSpec((S5_SLABS, 1024, LANES), const3),
            pl.BlockSpec((1, width), lambda g, t: (0, 0)),
            pl.BlockSpec((width, width), lambda g, t: (0, 0)),
            pl.BlockSpec((1, width), lambda g, t: (0, 0)),
        ],
        out_specs=[
            pl.BlockSpec((tm, width), lambda g, t: (g * tiles_per_group + t, 0)),
            pl.BlockSpec((n_seq, 2, S5_ROWS, LANES), lambda g, t: (g, 0, 0, 0)),
        ],
        out_shape=[SDS((m, width), BF16), SDS(h0.shape, F32)],
        scratch_shapes=[
            pltpu.VMEM((tm * 2 * S5_ROWS, LANES), F32),
            pltpu.VMEM((tm * 2 * S5_ROWS, LANES), F32),
            pltpu.VMEM((2, S5_ROWS, LANES), F32),
        ],
        compiler_params=_cparams(("parallel", "arbitrary")),
        name="s5_mix",
    )(proj, h0, lam, bw, cw, d, wg, bg)


def _pair_swap(y, half):
    n = y.shape[-1]
    lane = lax.broadcasted_iota(jnp.int32, y.shape, y.ndim - 1)
    up = pltpu.roll(y, n - half, y.ndim - 1)
    dn = pltpu.roll(y, half, y.ndim - 1)
    return jnp.where((lane % (2 * half)) < half, up, dn)


def _mla_prep_body(qn_ref, qr_ref, cl_ref, kr_ref, cos_ref, sin_ref, gq_ref, gqr_ref, glat_ref, gkr_ref,
                   wukt_ref, wuk_ref, grp_ref, sel_ref, selt_ref,
                   qlat_ref, qrope_ref, klat_ref, klatb_ref, krope_ref, kropeb_ref, krs_ref, krst_ref):
    half = D_ROPE // 2
    cos = cos_ref[...]
    sin = sin_ref[...]
    for h in range(MLA_HEADS):
        x = qn_ref[:, D_NOPE * h:D_NOPE * (h + 1)]
        y = _rms(x, gq_ref[...])
        ql = _dot(y.astype(BF16), wukt_ref[h]) * Q_SCALE
        qlat_ref[:, D_LATENT * h:D_LATENT * (h + 1)] = ql.astype(BF16)
    x = qr_ref[...]
    hi, lo = _split_bf16(x * x)
    ms = (_dot(hi, grp_ref[...]) + _dot(lo, grp_ref[...])) * (1.0 / D_ROPE)
    y = x * lax.rsqrt(ms + EPS) * gqr_ref[...]
    cos4 = jnp.concatenate([cos] * 4, axis=1)
    sin4 = jnp.concatenate([sin] * 4, axis=1)
    qrope_ref[...] = ((y * cos4 + _pair_swap(y, half) * sin4) * Q_SCALE).astype(BF16)
    cl = _rms(cl_ref[...], glat_ref[...])
    klat_ref[...] = cl
    clb = cl.astype(BF16)
    klatb_ref[...] = clb
    kf = _dot(clb, wuk_ref[...])
    hi, lo = _split_bf16(kf * kf)
    ss = _dot(hi, sel_ref[...]) + _dot(lo, sel_ref[...])
    krs_ref[...] = lax.rsqrt(ss[:, 0:MLA_HEADS] * (1.0 / D_NOPE) + EPS)
    sst = _dot_nt(selt_ref[...], hi) + _dot_nt(selt_ref[...], lo)
    krst_ref[...] = lax.rsqrt(sst * (1.0 / D_NOPE) + EPS)
    x = kr_ref[...]
    ms = jnp.sum(x * x, axis=-1, keepdims=True) * (1.0 / LANES)
    y = x * lax.rsqrt(ms + EPS) * gkr_ref[...]
    y = y * cos + _pair_swap(y, half) * sin
    krope_ref[...] = y[:, 0:D_ROPE]
    lane = lax.broadcasted_iota(jnp.int32, y.shape, 1)
    yb = y.astype(BF16)
    zero = jnp.zeros_like(yb)
    kropeb_ref[:, 0:LANES] = jnp.where(lane < D_ROPE, yb, zero)
    kropeb_ref[:, LANES:2 * LANES] = jnp.where(lane >= D_ROPE, yb, zero)


def mla_prep(proj, cos, sin, gq, gqr, glat, gkr, wukt, wuk, grp, sel, selt, tm):
    m = proj.shape[0]
    h = MLA_HEADS
    row = lambda i: (i, 0)
    c2 = lambda i: (0, 0)
    c3 = lambda i: (0, 0, 0)
    return pl.pallas_call(
        _mla_prep_body,
        grid=(m // tm,),
        in_specs=[
            pl.BlockSpec((tm, 1024), lambda i: (i, 1)),
            pl.BlockSpec((tm, 512), lambda i: (i, 4)),
            pl.BlockSpec((tm, 512), lambda i: (i, 5)),
            pl.BlockSpec((tm, LANES), lambda i: (i, 24)),
            pl.BlockSpec((tm, LANES), row),
            pl.BlockSpec((tm, LANES), row),
            pl.BlockSpec((1, D_NOPE), c2),
            pl.BlockSpec((1, 512), c2),
            pl.BlockSpec((1, D_LATENT), c2),
            pl.BlockSpec((1, LANES), c2),
            pl.BlockSpec((h, D_NOPE, D_LATENT), c3),
            pl.BlockSpec((D_LATENT, h * D_NOPE), c2),
            pl.BlockSpec((512, 512), c2),
            pl.BlockSpec((h * D_NOPE, LANES), c2),
            pl.BlockSpec((h, h * D_NOPE), c2),
        ],
        out_specs=[
            pl.BlockSpec((tm, h * D_LATENT), row),
            pl.BlockSpec((tm, 512), row),
            pl.BlockSpec((tm, D_LATENT), row),
            pl.BlockSpec((tm, D_LATENT), row),
            pl.BlockSpec((tm, D_ROPE), row),
            pl.BlockSpec((tm, 2 * LANES), row),
            pl.BlockSpec((tm, h), row),
            pl.BlockSpec((h, tm), lambda i: (0, i)),
        ],
        out_shape=[
            SDS((m, h * D_LATENT), BF16),
            SDS((m, 512), BF16),
            SDS((m, D_LATENT), F32),
            SDS((m, D_LATENT), BF16),
            SDS((m, D_ROPE), F32),
            SDS((m, 2 * LANES), BF16),
            SDS((m, h), F32),
            SDS((h, m), F32),
        ],
        compiler_params=_cparams(("parallel",)),
        name="mla_prep",
    )(proj, proj, proj, proj, cos, sin, gq, gqr, glat, gkr, wukt, wuk, grp, sel, selt)


def _attn_prompt_body(qi_ref, ki_ref, last_ref, ql_ref, qr_ref, kl_ref, kr_ref, rst_ref, wuv_ref,
                      o_ref, m_ref, l_ref, acc_ref, *, tq, tk):
    s_idx = pl.program_id(1)
    qi = qi_ref[s_idx]
    ki = ki_ref[s_idx]

    @pl.when(ki == 0)
    def _():
        m_ref[...] = jnp.full(m_ref.shape, NEG, F32)
        l_ref[...] = jnp.zeros(l_ref.shape, F32)
        acc_ref[...] = jnp.zeros(acc_ref.shape, F32)

    kl = kl_ref[...]
    rows = lax.broadcasted_iota(jnp.int32, (tq, tk), 0)
    cols = lax.broadcasted_iota(jnp.int32, (tq, tk), 1)
    keep = cols + (ki * tk - qi * tq) <= rows
    for h in range(MLA_HEADS):
        j, half = h % 4, h // 4
        s = _dot_nt(ql_ref[:, D_LATENT * h:D_LATENT * (h + 1)], kl) * rst_ref[h:h + 1, :]
        s = s + _dot_nt(qr_ref[:, LANES * j:LANES * (j + 1)], kr_ref[:, LANES * half:LANES * (half + 1)])
        s = jnp.where(keep, s, NEG)
        m_old = m_ref[h]
        m_new = jnp.maximum(m_old, jnp.max(s, axis=-1, keepdims=True))
        p = jnp.exp2(s - m_new)
        alpha = jnp.exp2(m_old - m_new)
        l_ref[h] = alpha * l_ref[h] + jnp.sum(p, axis=-1, keepdims=True)
        acc_ref[h] = alpha * acc_ref[h] + _dot(p.astype(BF16), kl)
        m_ref[h] = m_new

    @pl.when(last_ref[s_idx] == 1)
    def _():
        for h in range(MLA_HEADS):
            o_lat = acc_ref[h] / l_ref[h]
            o_ref[:, D_V * h:D_V * (h + 1)] = _dot(o_lat.astype(BF16), wuv_ref[h]).astype(o_ref.dtype)


def attn_prompt(qlat, qrope, klat, krope, krst, wuv, *, batch, seq, tq, tk):
    nq = seq // tq
    nk = seq // tk
    n_keys = [-(-((q + 1) * tq) // tk) for q in range(nq)]
    qi_tab = np.concatenate([np.full(n, q, np.int32) for q, n in enumerate(n_keys)])
    ki_tab = np.concatenate([np.arange(n, dtype=np.int32) for n in n_keys])
    last_tab = np.concatenate([(np.arange(n) == n - 1).astype(np.int32) for n in n_keys])
    n_tri = len(qi_tab)
    h = MLA_HEADS
    kern = functools.partial(_attn_prompt_body, tq=tq, tk=tk)
    grid_spec = pltpu.PrefetchScalarGridSpec(
        num_scalar_prefetch=3,
        grid=(batch, n_tri),
        in_specs=[
            pl.BlockSpec((tq, h * D_LATENT), lambda b, s, qi, ki, la: (b * nq + qi[s], 0)),
            pl.BlockSpec((tq, 512), lambda b, s, qi, ki, la: (b * nq + qi[s], 0)),
            pl.BlockSpec((tk, D_LATENT), lambda b, s, qi, ki, la: (b * nk + ki[s], 0)),
            pl.BlockSpec((tk, 2 * LANES), lambda b, s, qi, ki, la: (b * nk + ki[s], 0)),
            pl.BlockSpec((h, tk), lambda b, s, qi, ki, la: (0, b * nk + ki[s])),
            pl.BlockSpec((h, D_LATENT, D_V), lambda b, s, qi, ki, la: (0, 0, 0)),
        ],
        out_specs=pl.BlockSpec((tq, h * D_V), lambda b, s, qi, ki, la: (b * nq + qi[s], 0)),
        scratch_shapes=[
            pltpu.VMEM((h, tq, 1), F32),
            pltpu.VMEM((h, tq, 1), F32),
            pltpu.VMEM((h, tq, D_LATENT), F32),
        ],
    )
    return pl.pallas_call(
        kern,
        grid_spec=grid_spec,
        out_shape=SDS((batch * seq, h * D_V), BF16),
        compiler_params=_cparams(("parallel", "arbitrary")),
        name="attn_prompt",
    )(jnp.asarray(qi_tab), jnp.asarray(ki_tab), jnp.asarray(last_tab), qlat, qrope, klat, krope, krst, wuv)


def _attn_sample_body(pt_ref, ql_ref, qr_ref, *refs, n_pages_step, dec_seq):
    p_n = n_pages_step
    lat_refs = refs[0:p_n]
    rope_refs = refs[p_n:2 * p_n]
    rs_refs = refs[2 * p_n:3 * p_n]
    klo_ref, kro_ref, rso_ref, o_ref = refs[3 * p_n:3 * p_n + 4]
    scratch = refs[3 * p_n + 4:]
    n_st = len(scratch) // 3
    m_refs, l_refs, acc_refs = scratch[0:n_st], scratch[n_st:2 * n_st], scratch[2 * n_st:3 * n_st]
    j = pl.program_id(1)
    rows = dec_seq * MLA_HEADS

    @pl.when(j == 0)
    def _():
        for st in range(n_st):
            m_refs[st][...] = jnp.full(m_refs[st].shape, NEG, F32)
            l_refs[st][...] = jnp.zeros(l_refs[st].shape, F32)
            acc_refs[st][...] = jnp.zeros(acc_refs[st].shape, F32)

    ql = ql_ref[...]
    qr = qr_ref[...]

    def update(st, s, kl):
        m_old = m_refs[st][...]
        m_new = jnp.maximum(m_old, jnp.max(s, axis=-1, keepdims=True))
        p = jnp.exp2(s - m_new)
        alpha = jnp.exp2(m_old - m_new)
        l_refs[st][...] = alpha * l_refs[st][...] + jnp.sum(p, axis=-1, keepdims=True)
        acc_refs[st][...] = alpha * acc_refs[st][...] + _dot(p.astype(BF16), kl)
        m_refs[st][...] = m_new

    def scores(kl, krt, rst, n):
        s = _dot_nt(ql, kl)
        s = (s.reshape(dec_seq, MLA_HEADS, n) * rst[None]).reshape(rows, n)
        return s + _dot(qr, krt)

    per = p_n // n_st
    for st in range(n_st):
        grp = slice(st * per, (st + 1) * per)
        kl = jnp.concatenate([r[...].astype(BF16) for r in lat_refs[grp]], axis=0)
        krt = jnp.concatenate([r[...] for r in rope_refs[grp]], axis=1).astype(BF16)
        rst = jnp.concatenate([r[...] for r in rs_refs[grp]], axis=1)
        update(st, scores(kl, krt, rst, per * PAGE), kl)

    @pl.when(j == pl.num_programs(1) - 1)
    def _():
        klo = klo_ref[...]
        s = scores(klo, kro_ref[...], rso_ref[...], PAGE)
        tok = lax.broadcasted_iota(jnp.int32, (rows, PAGE), 0) // MLA_HEADS
        key = lax.broadcasted_iota(jnp.int32, (rows, PAGE), 1)
        s = jnp.where(key <= tok, s, NEG)
        update(0, s, klo)
        m = m_refs[0][...]
        for st in range(1, n_st):
            m = jnp.maximum(m, m_refs[st][...])
        acc = jnp.zeros(acc_refs[0].shape, F32)
        l = jnp.zeros(l_refs[0].shape, F32)
        for st in range(n_st):
            w = jnp.exp2(m_refs[st][...] - m)
            acc = acc + w * acc_refs[st][...]
            l = l + w * l_refs[st][...]
        o_ref[...] = (acc / l).astype(o_ref.dtype)


def attn_sample(page_table, qlat, qrope, cache_lat, cache_rope_t, cache_rs_t, klat_own, krope_own_t, rst_own,
                *, layer, n_pages_step):
    n_seq, n_pages = page_table.shape
    rows = qlat.shape[1]
    dec_seq = rows // MLA_HEADS
    p_n = n_pages_step
    n_steps = n_pages // p_n
    n_streams = 1
    kern = functools.partial(_attn_sample_body, n_pages_step=p_n, dec_seq=dec_seq)

    def page_spec(shape, i):
        return pl.BlockSpec((None, None) + shape,
                            lambda s, j, pt: (layer, pt[s * n_pages + j * p_n + i], 0, 0))

    own = lambda s, j, pt: (s, 0, 0)
    in_specs = [
        pl.BlockSpec((None, rows, D_LATENT), own),
        pl.BlockSpec((None, rows, D_ROPE), own),
    ]
    in_specs += [page_spec((PAGE, D_LATENT), i) for i in range(p_n)]
    in_specs += [page_spec((D_ROPE, PAGE), i) for i in range(p_n)]
    in_specs += [page_spec((MLA_HEADS, PAGE), i) for i in range(p_n)]
    in_specs += [
        pl.BlockSpec((None, PAGE, D_LATENT), own),
        pl.BlockSpec((None, D_ROPE, PAGE), own),
        pl.BlockSpec((None, MLA_HEADS, PAGE), own),
    ]
    grid_spec = pltpu.PrefetchScalarGridSpec(
        num_scalar_prefetch=1,
        grid=(n_seq, n_steps),
        in_specs=in_specs,
        out_specs=pl.BlockSpec((None, rows, D_LATENT), own),
        scratch_shapes=(
            [pltpu.VMEM((rows, 1), F32)] * (2 * n_streams) + [pltpu.VMEM((rows, D_LATENT), F32)] * n_streams
        ),
    )
    args = [page_table.reshape(-1), qlat, qrope]
    args += [cache_lat] * p_n + [cache_rope_t] * p_n + [cache_rs_t] * p_n
    args += [klat_own, krope_own_t, rst_own]
    return pl.pallas_call(
        kern,
        grid_spec=grid_spec,
        out_shape=SDS((n_seq, rows, D_LATENT), BF16),
        compiler_params=_cparams(("parallel", "arbitrary")),
        name="attn_sample",
    )(*args)


def _attn_paged_body(pt_ref, ql_ref, qr_ref, lat_hbm, rope_hbm, rs_hbm, klo_ref, kro_ref, rso_ref, o_ref,
                     lat_buf, rope_buf, rs_buf, sem, m_ref, l_ref, acc_ref, *, layer, n_pages, grp, dec_seq):
    seq = pl.program_id(0)
    n_seq = pl.num_programs(0)
    n_groups = n_pages // grp
    rows = dec_seq * MLA_HEADS

    def page_copies(pages, slot):
        cps = []
        for i in range(grp):
            cps.append(pltpu.make_async_copy(lat_hbm.at[layer, pages[i]], lat_buf.at[slot, i], sem.at[0, slot]))
            cps.append(pltpu.make_async_copy(rope_hbm.at[layer, pages[i]], rope_buf.at[slot, i], sem.at[1, slot]))
            cps.append(pltpu.make_async_copy(rs_hbm.at[layer, pages[i]], rs_buf.at[slot, i], sem.at[2, slot]))
        return cps

    def start_group(s, g, slot):
        base = s * n_pages + g * grp
        for c in page_copies([pt_ref[base + i] for i in range(grp)], slot):
            c.start()

    def wait_group(slot):
        for c in page_copies([0] * grp, slot):
            c.wait()

    @pl.when(seq == 0)
    def _():
        start_group(0, 0, 0)

    m_ref[...] = jnp.full(m_ref.shape, NEG, F32)
    l_ref[...] = jnp.zeros(l_ref.shape, F32)
    acc_ref[...] = jnp.zeros(acc_ref.shape, F32)
    ql = ql_ref[...]
    qr = qr_ref[...]

    def update(s, kl):
        m_old = m_ref[...]
        m_new = jnp.maximum(m_old, jnp.max(s, axis=-1, keepdims=True))
        p = jnp.exp2(s - m_new)
        alpha = jnp.exp2(m_old - m_new)
        l_ref[...] = alpha * l_ref[...] + jnp.sum(p, axis=-1, keepdims=True)
        acc_ref[...] = alpha * acc_ref[...] + _dot(p.astype(BF16), kl)
        m_ref[...] = m_new

    def scores(kl, krt, rst, n):
        s = _dot_nt(ql, kl)
        s = (s.reshape(dec_seq, MLA_HEADS, n) * rst[None]).reshape(rows, n)
        return s + _dot(qr, krt)

    def group(g, _):
        t = seq * n_groups + g
        slot = t % 2
        wait_group(slot)
        last_of_seq = g == n_groups - 1

        @pl.when(t + 1 < n_seq * n_groups)
        def _():
            start_group(jnp.where(last_of_seq, seq + 1, seq), jnp.where(last_of_seq, 0, g + 1), 1 - slot)

        tk = grp * PAGE
        kl = lat_buf[slot].reshape(tk, D_LATENT).astype(BF16)
        krt = jnp.concatenate([rope_buf[slot, i] for i in range(grp)], axis=1).astype(BF16)
        rst = jnp.concatenate([rs_buf[slot, i] for i in range(grp)], axis=1)
        update(scores(kl, krt, rst, tk), kl)
        return 0

    lax.fori_loop(0, n_groups, group, 0)

    klo = klo_ref[...]
    s = scores(klo, kro_ref[...], rso_ref[...], PAGE)
    tok = lax.broadcasted_iota(jnp.int32, (rows, PAGE), 0) // MLA_HEADS
    key = lax.broadcasted_iota(jnp.int32, (rows, PAGE), 1)
    s = jnp.where(key <= tok, s, NEG)
    update(s, klo)
    o_ref[...] = (acc_ref[...] / l_ref[...]).astype(o_ref.dtype)


def attn_paged(page_table, qlat, qrope, cache_lat, cache_rope_t, cache_rs_t, klat_own, krope_own_t, rst_own,
               *, layer, grp):
    n_seq, n_pages = page_table.shape
    rows = qlat.shape[1]
    dec_seq = rows // MLA_HEADS
    kern = functools.partial(_attn_paged_body, layer=layer, n_pages=n_pages, grp=grp, dec_seq=dec_seq)
    own = lambda s, pt: (s, 0, 0)
    hbm = pl.BlockSpec(memory_space=pl.ANY)
    grid_spec = pltpu.PrefetchScalarGridSpec(
        num_scalar_prefetch=1,
        grid=(n_seq,),
        in_specs=[
            pl.BlockSpec((None, rows, D_LATENT), own),
            pl.BlockSpec((None, rows, D_ROPE), own),
            hbm, hbm, hbm,
            pl.BlockSpec((None, PAGE, D_LATENT), own),
            pl.BlockSpec((None, D_ROPE, PAGE), own),
            pl.BlockSpec((None, MLA_HEADS, PAGE), own),
        ],
        out_specs=pl.BlockSpec((None, rows, D_LATENT), own),
        scratch_shapes=[
            pltpu.VMEM((2, grp, PAGE, D_LATENT), F32),
            pltpu.VMEM((2, grp, D_ROPE, PAGE), F32),
            pltpu.VMEM((2, grp, MLA_HEADS, PAGE), F32),
            pltpu.SemaphoreType.DMA((3, 2)),
            pltpu.VMEM((rows, 1), F32),
            pltpu.VMEM((rows, 1), F32),
            pltpu.VMEM((rows, D_LATENT), F32),
        ],
    )
    return pl.pallas_call(
        kern,
        grid_spec=grid_spec,
        out_shape=SDS((n_seq, rows, D_LATENT), BF16),
        compiler_params=_cparams(("arbitrary",)),
        name="attn_paged",
    )(page_table.reshape(-1), qlat, qrope, cache_lat, cache_rope_t, cache_rs_t, klat_own, krope_own_t, rst_own)


def _head_matmul_body(x_ref, w_ref, o_ref):
    o_ref[...] = _dot(x_ref[...], w_ref[...]).astype(o_ref.dtype)


def head_matmul(x, w):
    m = x.shape[0]
    h, k, n = w.shape
    return pl.pallas_call(
        _head_matmul_body,
        grid=(h,),
        in_specs=[pl.BlockSpec((m, k), lambda i: (0, i)), pl.BlockSpec((None, k, n), lambda i: (i, 0, 0))],
        out_specs=pl.BlockSpec((m, n), lambda i: (0, i)),
        out_shape=SDS((m, h * n), BF16),
        compiler_params=_cparams(("parallel",)),
        name="head_matmul",
    )(x, w)


def _cumsum_rows(x, c):
    r = lax.broadcasted_iota(jnp.int32, (c, c), 0)
    s = lax.broadcasted_iota(jnp.int32, (c, c), 1)
    tri = (s <= r).astype(BF16)
    hi, lo = _split_bf16(x)
    return _dot(tri, hi) + _dot(tri, lo)


def _gla_body(q_ref, k_ref, v_ref, gg_ref, sm_ref, wa_ref, ba_ref, gn_ref, s0_ref,
              o_ref, so_ref, st_ref, b_ref, oi_ref, *, c, inner, carried):
    ci = pl.program_id(2)
    nb = c // SUBLANES
    sub = lax.broadcasted_iota(jnp.int32, (SUBLANES, 1), 0)

    if carried:
        @pl.when(ci == 0)
        def _():
            st_ref[...] = s0_ref[0].T

    def unit(u, _):
        r0 = pl.multiple_of(u * c, c)
        rows = pl.ds(r0, c)
        st = st_ref[...] if carried else s0_ref[u].T
        x = _dot(sm_ref[rows, :].astype(BF16), wa_ref[...]) + ba_ref[...]
        g = _log_sigmoid(x) * (1.0 / GLA_GATE_NORM)
        b = _cumsum_rows(g, c)
        b_ref[...] = b
        q = q_ref[rows, :]
        k = k_ref[rows, :]
        v = v_ref[rows, :]
        o_inter = _dot_nt((q * jnp.exp(b)).astype(BF16), st.astype(BF16))

        for bi in range(nb):
            accs = [jnp.zeros((SUBLANES, GLA_DV), F32)] * SUBLANES
            for bj in range(bi + 1):
                kj = k_ref[pl.ds(r0 + SUBLANES * bj, SUBLANES), :]
                vj = v_ref[pl.ds(r0 + SUBLANES * bj, SUBLANES), :]
                bj_rows = b_ref[SUBLANES * bj:SUBLANES * (bj + 1), :]
                for tt in range(SUBLANES):
                    t = SUBLANES * bi + tt
                    qt = q_ref[pl.ds(r0 + t, 1), :]
                    dlt = b_ref[t:t + 1, :] - bj_rows
                    if bj == bi:
                        dlt = jnp.where(sub <= tt, dlt, NEG)
                    a = jnp.sum(jnp.exp(dlt) * kj * qt, axis=-1, keepdims=True)
                    accs[tt] = accs[tt] + a * vj
            for tt in range(SUBLANES):
                t = SUBLANES * bi + tt
                oi_ref[t:t + 1, :] = jnp.sum(accs[tt], axis=0, keepdims=True)

        o = (o_inter + oi_ref[...]) * (GLA_DK ** -0.5)
        o = _rms(o, gn_ref[...])
        gg = gg_ref[rows, :]
        o_ref[rows, :] = (o * (gg * jax.nn.sigmoid(gg))).astype(o_ref.dtype)

        b_last = b[c - 1:c, :]
        kd = k * jnp.exp(b_last - b)
        st_new = st * jnp.exp(b_last) + _dot_tn(v.astype(BF16), kd.astype(BF16))
        if carried:
            st_ref[...] = st_new
        else:
            so_ref[u] = st_new.T
        return 0

    lax.fori_loop(0, inner, unit, 0)

    if carried:
        @pl.when(ci == pl.num_programs(2) - 1)
        def _():
            so_ref[0] = st_ref[...].T


def gla_mix(proj, wa, ba, gn, s0, *, n_seq, n_chunks, c):
    m = proj.shape[0]
    hh = GLA_HEADS
    carried = n_chunks > 1
    if carried:
        inner = _pick(n_chunks, (4, 2, 1))
        seqs, steps = 1, n_chunks // inner
    else:
        inner = _pick(n_seq, (16, 8, 4, 2, 1))
        seqs, steps = inner, 1
    rows = inner * c
    kern = functools.partial(_gla_body, c=c, inner=inner, carried=carried)
    row = lambda g, h, ci: g * steps + ci
    st4 = lambda g, h, ci: (g, h, 0, 0)
    return pl.pallas_call(
        kern,
        grid=(n_seq // seqs, hh, steps),
        in_specs=[
            pl.BlockSpec((rows, GLA_DK), lambda g, h, ci: (row(g, h, ci), h)),
            pl.BlockSpec((rows, GLA_DK), lambda g, h, ci: (row(g, h, ci), 4 + h)),
            pl.BlockSpec((rows, GLA_DV), lambda g, h, ci: (row(g, h, ci), 4 + h)),
            pl.BlockSpec((rows, GLA_DV), lambda g, h, ci: (row(g, h, ci), 8 + h)),
            pl.BlockSpec((rows, LANES), lambda g, h, ci: (row(g, h, ci), 48)),
            pl.BlockSpec((LANES, GLA_DK), lambda g, h, ci: (0, h)),
            pl.BlockSpec((1, GLA_DK), lambda g, h, ci: (0, h)),
            pl.BlockSpec((1, GLA_DV), lambda g, h, ci: (0, 0)),
            pl.BlockSpec((seqs, None, GLA_DK, GLA_DV), st4),
        ],
        out_specs=[
            pl.BlockSpec((rows, GLA_DV), lambda g, h, ci: (row(g, h, ci), h)),
            pl.BlockSpec((seqs, None, GLA_DK, GLA_DV), st4),
        ],
        out_shape=[SDS((m, hh * GLA_DV), BF16), SDS((n_seq, hh, GLA_DK, GLA_DV), F32)],
        scratch_shapes=[
            pltpu.VMEM((GLA_DV, GLA_DK), F32),
            pltpu.VMEM((c, GLA_DK), F32),
            pltpu.VMEM((c, GLA_DV), F32),
        ],
        compiler_params=_cparams(("parallel", "parallel", "arbitrary")),
        name="gla_mix",
    )(proj, proj, proj, proj, proj, wa, ba, gn, s0)


ML_I_LANE = GLA_RANK
ML_F_LANE = GLA_RANK + ML_HEADS


def _mlstm_body(q_ref, k_ref, v_ref, mo_ref, sm_ref, bias_ref, gn_ref, c0_ref, n0_ref, m0_ref,
                o_ref, co_ref, no_ref, mo_out_ref, cs_ref, ns_ref, ms_ref, tr_ref, *, c, inner, carried):
    ci = pl.program_id(1)
    r = lax.broadcasted_iota(jnp.int32, (c, c), 0)
    s = lax.broadcasted_iota(jnp.int32, (c, c), 1)
    causal = s <= r
    lane_h = lax.broadcasted_iota(jnp.int32, (1, ML_HEADS), 1)

    if carried:
        @pl.when(ci == 0)
        def _():
            cs_ref[...] = c0_ref[0]
            ns_ref[...] = n0_ref[0]
            ms_ref[...] = m0_ref[0]

    def unit(u, _):
        r0 = pl.multiple_of(u * c, c)
        rows = pl.ds(r0, c)
        pre = sm_ref[rows, :] + bias_ref[...]
        bcum = _cumsum_rows(_log_sigmoid(pre), c)
        tr_ref[0] = pre.T
        tr_ref[1] = bcum.T
        m_all = ms_ref[...] if carried else m0_ref[u]
        m_row = jnp.zeros((1, ML_HEADS), F32)
        for h in range(ML_HEADS):
            li, lf = ML_I_LANE + h, ML_F_LANE + h
            i_col = pre[:, li:li + 1]
            b_col = bcum[:, lf:lf + 1]
            i_row = tr_ref[0, li:li + 1, :]
            b_row = tr_ref[1, lf:lf + 1, :]
            m_prev = m_all[:, h:h + 1]
            rel = jnp.where(causal, b_col - b_row + i_row, NEG)
            carry_log = b_col + m_prev
            m_t = jnp.maximum(carry_log, jnp.max(rel, axis=-1, keepdims=True))
            w_intra = jnp.exp(rel - m_t)
            w_carry = jnp.exp(carry_log - m_t)

            q = q_ref[rows, ML_DK * h:ML_DK * (h + 1)] * (ML_DK ** -0.5)
            k = k_ref[rows, ML_DK * h:ML_DK * (h + 1)]
            v = v_ref[rows, ML_DV * h:ML_DV * (h + 1)]
            qb = q.astype(BF16)
            vb = v.astype(BF16)
            cm = cs_ref[h] if carried else c0_ref[u, h]
            nv = ns_ref[h:h + 1, :] if carried else n0_ref[u, h:h + 1, :]
            qk = _dot_nt(qb, k.astype(BF16)) * w_intra
            num = _dot(qk.astype(BF16), vb) + w_carry * _dot(qb, cm.astype(BF16))
            den = jnp.sum(qk, axis=-1, keepdims=True) + w_carry * jnp.sum(q * nv, axis=-1, keepdims=True)
            hh = num / jnp.maximum(jnp.abs(den), jnp.exp(-m_t))

            mo = mo_ref[rows, ML_DV * h:ML_DV * (h + 1)]
            o_ref[rows, ML_DV * h:ML_DV * (h + 1)] = (jax.nn.sigmoid(mo) * _rms(hh, gn_ref[...])).astype(o_ref.dtype)

            m_new = m_t[c - 1:c, :]
            b_last = b_col[c - 1:c, :]
            w_state = jnp.exp(b_last - b_col + i_col - m_new)
            decay = jnp.exp(b_last + m_prev - m_new)
            kw = k * w_state
            cm_new = decay * cm + _dot_tn(kw.astype(BF16), vb)
            nv_new = decay * nv + jnp.sum(kw, axis=0, keepdims=True)
            if carried:
                cs_ref[h] = cm_new
                ns_ref[h:h + 1, :] = nv_new
            else:
                co_ref[u, h] = cm_new
                no_ref[u, h:h + 1, :] = nv_new
            m_row = jnp.where(lane_h == h, m_new, m_row)
        if carried:
            ms_ref[...] = m_row
        else:
            mo_out_ref[u] = m_row
        return 0

    lax.fori_loop(0, inner, unit, 0)

    if carried:
        @pl.when(ci == pl.num_programs(1) - 1)
        def _():
            co_ref[0] = cs_ref[...]
            no_ref[0] = ns_ref[...]
            mo_out_ref[0] = ms_ref[...]


def mlstm_mix(proj, bias, gn, c0, n0, m0, *, n_seq, n_chunks, c):
    m = proj.shape[0]
    hh = ML_HEADS
    carried = n_chunks > 1
    if carried:
        inner = _pick(n_chunks, (4, 2, 1))
        seqs, steps = 1, n_chunks // inner
    else:
        inner = _pick(n_seq, (8, 4, 2, 1))
        seqs, steps = inner, 1
    rows = inner * c
    kern = functools.partial(_mlstm_body, c=c, inner=inner, carried=carried)
    row = lambda g, ci: g * steps + ci
    st4 = lambda g, ci: (g, 0, 0, 0)
    st3 = lambda g, ci: (g, 0, 0)
    return pl.pallas_call(
        kern,
        grid=(n_seq // seqs, steps),
        in_specs=[
            pl.BlockSpec((rows, hh * ML_DK), lambda g, ci: (row(g, ci), 6)),
            pl.BlockSpec((rows, hh * ML_DK), lambda g, ci: (row(g, ci), 7)),
            pl.BlockSpec((rows, hh * ML_DV), lambda g, ci: (row(g, ci), 4)),
            pl.BlockSpec((rows, hh * ML_DV), lambda g, ci: (row(g, ci), 5)),
            pl.BlockSpec((rows, LANES), lambda g, ci: (row(g, ci), 48)),
            pl.BlockSpec((1, LANES), lambda g, ci: (0, 0)),
            pl.BlockSpec((1, ML_DV), lambda g, ci: (0, 0)),
            pl.BlockSpec((seqs, hh, ML_DK, ML_DV), st4),
            pl.BlockSpec((seqs, hh, ML_DK), st3),
            pl.BlockSpec((seqs, 1, hh), st3),
        ],
        out_specs=[
            pl.BlockSpec((rows, hh * ML_DV), lambda g, ci: (row(g, ci), 0)),
            pl.BlockSpec((seqs, hh, ML_DK, ML_DV), st4),
            pl.BlockSpec((seqs, hh, ML_DK), st3),
            pl.BlockSpec((seqs, 1, hh), st3),
        ],
        out_shape=[
            SDS((m, hh * ML_DV), BF16),
            SDS((n_seq, hh, ML_DK, ML_DV), F32),
            SDS((n_seq, hh, ML_DK), F32),
            SDS((n_seq, 1, hh), F32),
        ],
        scratch_shapes=[
            pltpu.VMEM((hh, ML_DK, ML_DV), F32),
            pltpu.VMEM((hh, ML_DK), F32),
            pltpu.VMEM((1, hh), F32),
            pltpu.VMEM((2, LANES, c), F32),
        ],
        compiler_params=_cparams(("parallel", "arbitrary")),
        name="mlstm_mix",
    )(proj, proj, proj, proj, proj, bias, gn, c0, n0, m0)


def _pad_cols(w, n):
    return jnp.pad(w, ((0, 0), (0, n - w.shape[1])))


def _even_in_weight(w):
    d = w.shape[0]
    hq = MLA_HEADS * (D_NOPE + D_ROPE)
    u = w[:, :1024]
    q = w[:, 1024:1024 + hq].reshape(d, MLA_HEADS, D_NOPE + D_ROPE)
    q_nope = q[:, :, :D_NOPE].reshape(d, MLA_HEADS * D_NOPE)
    q_rope = q[:, :, D_NOPE:].reshape(d, 2, 4, D_ROPE).transpose(0, 2, 1, 3).reshape(d, MLA_HEADS * D_ROPE)
    c_lat = w[:, 1024 + hq:1024 + hq + D_LATENT]
    k_r = w[:, 1024 + hq + D_LATENT:]
    return _pad_cols(jnp.concatenate([u, q_nope, q_rope, c_lat, k_r, k_r], axis=1), 3328).astype(BF16)


def _odd_in_weight(w):
    sizes = (512, 512, 1024, GLA_RANK, 1024, 512, 512, 1024, ML_HEADS, ML_HEADS, 1024)
    offs = np.concatenate([[0], np.cumsum(sizes)])
    seg = [w[:, offs[i]:offs[i + 1]] for i in range(len(sizes))]
    gq, gk, gv, ga, gg, mq, mk, mv, mi, mf, mo = seg
    small = _pad_cols(jnp.concatenate([ga, mi, mf], axis=1), LANES)
    return _pad_cols(jnp.concatenate([gq, gk, gv, gg, mq, mk, mv, mo, small], axis=1), 6400).astype(BF16)


def _s5_params(a_re, a_im, log_dt, b_re, b_im, c_re, c_im):
    g, p = a_re.shape
    ar = jnp.minimum(a_re, -1e-4)
    ai = a_im
    dt = jnp.exp(log_dt)[:, None]
    mag = jnp.exp(ar * dt)
    lr = mag * jnp.cos(ai * dt)
    li = mag * jnp.sin(ai * dt)
    den = ar * ar + ai * ai
    cr = ((lr - 1.0) * ar + li * ai) / den
    ci = (li * ar - (lr - 1.0) * ai) / den
    bb_re = cr[..., None] * b_re - ci[..., None] * b_im
    bb_im = cr[..., None] * b_im + ci[..., None] * b_re
    eye = jnp.eye(8, dtype=F32)

    def in_slab(bb):
        x = bb.transpose(0, 2, 1).reshape(S5_SLABS, 8, S5_GROUP, p)
        return jnp.einsum("jaip,ab->jaibp", x, eye).reshape(S5_SLABS, 8 * S5_GROUP, 8 * p)

    def out_slab(cc):
        x = cc.transpose(0, 2, 1).reshape(S5_SLABS, 8, p, S5_GROUP)
        return jnp.einsum("japi,ab->japbi", x, eye).reshape(S5_SLABS, 8 * p, 8 * S5_GROUP)

    bw = jnp.concatenate([in_slab(bb_re), in_slab(bb_im)], axis=2).astype(BF16)
    cw = jnp.concatenate([out_slab(c_re), -out_slab(c_im)], axis=1).astype(BF16)
    lam = jnp.stack([lr.reshape(S5_ROWS, LANES), li.reshape(S5_ROWS, LANES)])
    return lam, bw, cw


def _rope_tables(pos):
    half = D_ROPE // 2
    inv_freq = ROPE_THETA ** (-jnp.arange(half, dtype=F32) / half)
    ang = pos.astype(F32)[:, None] * inv_freq
    cos, sin = jnp.cos(ang), jnp.sin(ang)
    return jnp.tile(cos, (1, 4)), jnp.tile(jnp.concatenate([-sin, sin], axis=1), (1, 2))


def _pick(n, prefs):
    for t in prefs:
        if n % t == 0:
            return t
    return n


def _even_layer(x, *, n_seq, seq_len, pos, h0_re, h0_im, attend, p):
    m = x.shape[0]
    tm = _pick(m, (512, 256))
    proj = norm_matmul(x, p["norm_mix"], p["w_in"], tm, 1664)
    if seq_len >= 256:
        tile = _pick(seq_len, (256,))
        s5_kw = dict(n_groups=n_seq, tiles_per_group=seq_len // tile, tm=tile, n_seq=1, seq_len=tile)
    else:
        per = max(1, 256 // seq_len)
        per = _pick(n_seq, (per, 16, 8, 4, 2, 1))
        s5_kw = dict(n_groups=n_seq // per, tiles_per_group=1, tm=per * seq_len, n_seq=per, seq_len=seq_len)
    h0 = jnp.stack([h0_re.reshape(n_seq, S5_ROWS, LANES), h0_im.reshape(n_seq, S5_ROWS, LANES)], axis=1)
    s5_y, h_last = s5_mix(proj, h0, p["lam"], p["bw"], p["cw"], p["s5_d"], p["w_glu"], p["b_glu"], **s5_kw)
    s5_re = h_last[:, 0].reshape(n_seq, S5_ROWS * LANES // S5_STATE, S5_STATE)
    s5_im = h_last[:, 1].reshape(n_seq, S5_ROWS * LANES // S5_STATE, S5_STATE)
    cos, sin = _rope_tables(pos)
    qlat, qrope, klat, klat_b, krope, krope_b, krs, krst = mla_prep(
        proj, cos, sin, p["g_q"], p["g_qr"], p["g_lat"], p["g_kr"], p["wukt"], p["wuk"], p["grp"], p["sel"],
        p["selt"], _pick(m, (256,)))
    o = attend(qlat, qrope, klat_b, krope_b, krst)
    x = out_proj(x, s5_y, o, p["w_out_a"], p["w_out_b"], tm)
    return x, (klat, krope, krs, s5_re, s5_im)


def _odd_layer(x, *, n_seq, seq_len, s0, c0, n0, m0, p):
    m = x.shape[0]
    tm = _pick(m, (512, 256))
    proj = norm_matmul(x, p["norm_mix"], p["w_in"], tm, 1280)
    c = math.gcd(seq_len, CHUNK)
    n_chunks = seq_len // c
    o_gla, s_gla = gla_mix(proj, p["wa"], p["ba"], p["g_gla"], s0, n_seq=n_seq, n_chunks=n_chunks, c=c)
    h_ml, cm, nv, mm = mlstm_mix(proj, p["ml_bias"], p["g_ml"], c0, n0, m0[:, None, :],
                                 n_seq=n_seq, n_chunks=n_chunks, c=c)
    x = out_proj(x, o_gla, h_ml, p["w_out_a"], p["w_out_b"], tm)
    return x, (s_gla, cm, nv, mm[:, 0, :])


def kernel(x_prompt, x_sample, cache_mla_latent, cache_mla_k_rope, cache_mla_k_rscale, state_s5_re, state_s5_im, state_gla, state_mlstm_c, state_mlstm_n, state_mlstm_m, page_table, norm_mix, norm_ffn, ffn_w1, ffn_w3, ffn_w2, e_w_in, s5_a_re, s5_a_im, s5_log_dt, s5_b_re, s5_b_im, s5_c_re, s5_c_im, s5_d, s5_w_glu, s5_b_glu, mla_g_qnope, mla_g_qrope, mla_g_knope, mla_g_krope, mla_g_latent, mla_w_uk, mla_w_uv, e_w_out, o_w_in, gla_w_a2, gla_b_a, gla_g_norm, ml_b_i, ml_b_f, ml_g_norm, o_w_out):
    bp, lp, d = x_prompt.shape
    bs, ls, _ = x_sample.shape
    depth = norm_mix.shape[0]
    past_len = page_table.shape[1] * PAGE
    pos_p = jnp.arange(lp, dtype=jnp.int32)
    pos_s = past_len + jnp.arange(ls, dtype=jnp.int32)
    xp = x_prompt.reshape(bp * lp, d)
    xs = x_sample.reshape(bs * ls, d)
    even_p, even_s, odd_p, odd_s = [], [], [], []
    w1b, w3b, w2b = ffn_w1.astype(BF16), ffn_w3.astype(BF16), ffn_w2.astype(BF16)
    for l in range(depth):
        j = l // 2
        if l % 2 == 0:
            lam, bw, cw = _s5_params(s5_a_re[j], s5_a_im[j], s5_log_dt[j], s5_b_re[j], s5_b_im[j],
                                     s5_c_re[j], s5_c_im[j])
            s5w = s5_a_re.shape[1] * S5_GROUP
            h = MLA_HEADS
            wuk = mla_w_uk[j]
            sel = (jnp.arange(h * D_NOPE)[:, None] // D_NOPE == jnp.arange(LANES)[None, :]).astype(BF16)
            gi = jnp.arange(512) // D_ROPE
            p = dict(
                norm_mix=norm_mix[l][None], w_in=_even_in_weight(e_w_in[j]),
                lam=lam, bw=bw, cw=cw, s5_d=s5_d[j][None], w_glu=s5_w_glu[j].astype(BF16), b_glu=s5_b_glu[j][None],
                g_q=(mla_g_qnope[j] * mla_g_knope[j])[None], g_qr=jnp.tile(mla_g_qrope[j], h)[None],
                g_lat=mla_g_latent[j][None], g_kr=jnp.tile(mla_g_krope[j], 2)[None],
                wukt=wuk.transpose(1, 2, 0).astype(BF16), wuk=wuk.reshape(D_LATENT, h * D_NOPE).astype(BF16),
                grp=(gi[:, None] == gi[None, :]).astype(BF16), sel=sel, selt=sel[:, :h].T,
                w_out_a=e_w_out[j][:s5w].astype(BF16), w_out_b=e_w_out[j][s5w:].astype(BF16),
            )
            wuv = mla_w_uv[j].transpose(1, 0, 2).astype(BF16)

            def attend_p(qlat, qrope, klat_b, krope_b, krst):
                return attn_prompt(qlat, qrope, klat_b, krope_b, krst, wuv, batch=bp, seq=lp,
                                   tq=_pick(lp, (256,)), tk=_pick(lp, (1024, 512, 256)))

            def attend_s(qlat, qrope, klat_b, krope_b, krst):
                rows = ls * h
                qr = qrope.reshape(bs * ls, 4, 2, D_ROPE).transpose(0, 2, 1, 3).reshape(bs, rows, D_ROPE)
                klat_own = jnp.pad(klat_b.reshape(bs, ls, D_LATENT), ((0, 0), (0, PAGE - ls), (0, 0)))
                lane_pad = ((0, 0), (0, 0), (0, PAGE - ls))
                krope_own_t = jnp.pad(krope_b[:, :D_ROPE].reshape(bs, ls, D_ROPE).transpose(0, 2, 1), lane_pad)
                rst_own = jnp.pad(krst.reshape(h, bs, ls).transpose(1, 0, 2), lane_pad)
                n_pages = page_table.shape[1]
                o_lat = attn_paged(page_table, qlat.reshape(bs, rows, D_LATENT), qr, cache_mla_latent,
                                   jnp.swapaxes(cache_mla_k_rope, 2, 3), jnp.swapaxes(cache_mla_k_rscale, 2, 3),
                                   klat_own, krope_own_t, rst_own,
                                   layer=j, grp=_pick(n_pages, (32, 16, 8, 4, 2, 1)))
                return head_matmul(o_lat.reshape(bs * ls, h * D_LATENT), wuv)

            zeros = jnp.zeros((bp, s5_a_re.shape[1], S5_STATE), F32)
            xp, st_p = _even_layer(xp, n_seq=bp, seq_len=lp, pos=jnp.tile(pos_p, bp), h0_re=zeros, h0_im=zeros,
                                   attend=attend_p, p=p)
            xs, st_s = _even_layer(xs, n_seq=bs, seq_len=ls, pos=jnp.tile(pos_s, bs), h0_re=state_s5_re[j],
                                   h0_im=state_s5_im[j], attend=attend_s, p=p)
            even_p.append((st_p[0].reshape(bp, lp, -1), st_p[1].reshape(bp, lp, -1), st_p[2].reshape(bp, lp, -1),
                           st_p[3], st_p[4]))
            even_s.append((st_s[0].reshape(bs, ls, -1), st_s[1].reshape(bs, ls, -1), st_s[2].reshape(bs, ls, -1),
                           st_s[3], st_s[4]))
        else:
            hk = GLA_HEADS * GLA_DK
            wo = o_w_out[j]
            bias = jnp.zeros((LANES,), F32)
            bias = bias.at[ML_I_LANE:ML_I_LANE + ML_HEADS].set(ml_b_i[j]).at[ML_F_LANE:ML_F_LANE + ML_HEADS].set(ml_b_f[j])
            p = dict(
                norm_mix=norm_mix[l][None], w_in=_odd_in_weight(o_w_in[j]),
                wa=jnp.pad(gla_w_a2[j], ((0, LANES - GLA_RANK), (0, 0))).astype(BF16), ba=gla_b_a[j][None],
                g_gla=gla_g_norm[j][None], ml_bias=bias[None], g_ml=ml_g_norm[j][None],
                w_out_a=wo[:GLA_HEADS * GLA_DV].astype(BF16), w_out_b=wo[GLA_HEADS * GLA_DV:].astype(BF16),
            )
            del hk
            xp, st_p = _odd_layer(xp, n_seq=bp, seq_len=lp, s0=jnp.zeros((bp, GLA_HEADS, GLA_DK, GLA_DV), F32),
                                  c0=jnp.zeros((bp, ML_HEADS, ML_DK, ML_DV), F32),
                                  n0=jnp.zeros((bp, ML_HEADS, ML_DK), F32), m0=jnp.zeros((bp, ML_HEADS), F32), p=p)
            xs, st_s = _odd_layer(xs, n_seq=bs, seq_len=ls, s0=state_gla[j], c0=state_mlstm_c[j],
                                  n0=state_mlstm_n[j], m0=state_mlstm_m[j], p=p)
            odd_p.append(st_p)
            odd_s.append(st_s)
        g = norm_ffn[l][None]
        tf = _pick(w1b.shape[2], (512, 256, 128))
        xp = ffn(xp, g, w1b, w3b, w2b, l, _pick(xp.shape[0], (1024, 512, 256)), tf)
        xs = ffn(xs, g, w1b, w3b, w2b, l, _pick(xs.shape[0], (1024, 512, 256)), tf)
    p_lat, p_rope, p_rscale, p_s5_re, p_s5_im = [jnp.stack(t) for t in zip(*even_p)]
    s_lat, s_rope, s_rscale, s_s5_re, s_s5_im = [jnp.stack(t) for t in zip(*even_s)]
    p_gla, p_ml_c, p_ml_n, p_ml_m = [jnp.stack(t) for t in zip(*odd_p)]
    s_gla, s_ml_c, s_ml_n, s_ml_m = [jnp.stack(t) for t in zip(*odd_s)]
    return (xp.reshape(bp, lp, d), xs.reshape(bs, ls, d),
            p_lat, p_rope, p_rscale, p_s5_re, p_s5_im, p_gla, p_ml_c, p_ml_n, p_ml_m,
            s_lat, s_rope, s_rscale, s_s5_re, s_s5_im, s_gla, s_ml_c, s_ml_n, s_ml_m)
```

```python
import functools
import math

import jax
import jax.numpy as jnp
import numpy as np
from jax import lax
from jax.experimental import pallas as pl
from jax.experimental.pallas import tpu as pltpu

F32, BF16 = jnp.float32, jnp.bfloat16
SDS = jax.ShapeDtypeStruct

EPS = 1e-6
LANES = 128
SUBLANES = 8
VMEM_LIMIT = 52 * 1024 * 1024

S5_GROUP = 16
S5_STATE = 64
MLA_HEADS = 8
D_NOPE = 128
D_ROPE = 64
D_V = 128
D_LATENT = 512
ROPE_THETA = 10000.0
MLA_SCALE = (D_NOPE + D_ROPE) ** -0.5
Q_SCALE = MLA_SCALE * math.log2(math.e)
PAGE = 128
GLA_HEADS = 4
GLA_DK = 128
GLA_DV = 256
GLA_RANK = 16
GLA_GATE_NORM = 16.0
ML_HEADS = 4
ML_DK = 128
ML_DV = 256
CHUNK = 64
NEG = -1e30
Q_TILE = 256


def _cparams(sem):
    return pltpu.CompilerParams(dimension_semantics=sem, vmem_limit_bytes=VMEM_LIMIT)


def _rms(x, g):
    ms = jnp.mean(x * x, axis=-1, keepdims=True)
    return x * lax.rsqrt(ms + EPS) * g


def _dot(a, b):
    return jnp.dot(a, b, preferred_element_type=F32)


def _dot_nt(a, b):
    return lax.dot_general(a, b, (((1,), (1,)), ((), ())), preferred_element_type=F32)


def _dot_tn(a, b):
    return lax.dot_general(a, b, (((0,), (0,)), ((), ())), preferred_element_type=F32)


def _split_bf16(x):
    hi = x.astype(BF16)
    lo = (x - hi.astype(F32)).astype(BF16)
    return hi, lo


def _log_sigmoid(x):
    return jnp.minimum(x, 0.0) - jnp.log(1.0 + jnp.exp(-jnp.abs(x)))


def _norm_matmul_body(x_ref, g_ref, w_ref, o_ref, xn_ref):
    @pl.when(pl.program_id(1) == 0)
    def _():
        xn_ref[...] = _rms(x_ref[...], g_ref[...]).astype(BF16)

    o_ref[...] = _dot(xn_ref[...], w_ref[...])


def norm_matmul(x, g, w, tm, tn):
    m, k = x.shape
    n = w.shape[1]
    return pl.pallas_call(
        _norm_matmul_body,
        grid=(m // tm, n // tn),
        in_specs=[
            pl.BlockSpec((tm, k), lambda i, j: (i, 0)),
            pl.BlockSpec((1, k), lambda i, j: (0, 0)),
            pl.BlockSpec((k, tn), lambda i, j: (0, j)),
        ],
        out_specs=pl.BlockSpec((tm, tn), lambda i, j: (i, j)),
        out_shape=SDS((m, n), F32),
        scratch_shapes=[pltpu.VMEM((tm, k), BF16)],
        compiler_params=_cparams(("parallel", "arbitrary")),
        name="norm_matmul",
    )(x, g, w)


def _ffn_body(x_ref, g_ref, w1_ref, w3_ref, w2_ref, o_ref, xn_ref):
    @pl.when(pl.program_id(1) == 0)
    def _():
        x = x_ref[...]
        xn_ref[...] = _rms(x, g_ref[...]).astype(BF16)
        o_ref[...] = x

    xn = xn_ref[...]
    a = _dot(xn, w1_ref[...])
    b = _dot(xn, w3_ref[...])
    h = (a * jax.nn.sigmoid(a) * b).astype(BF16)
    o_ref[...] += _dot(h, w2_ref[...])


def ffn(x, g, w1, w3, w2, layer, tm, tf):
    m, d = x.shape
    f = w1.shape[2]
    return pl.pallas_call(
        _ffn_body,
        grid=(m // tm, f // tf),
        in_specs=[
            pl.BlockSpec((tm, d), lambda i, j: (i, 0), pipeline_mode=pl.Buffered(1)),
            pl.BlockSpec((1, d), lambda i, j: (0, 0)),
            pl.BlockSpec((None, d, tf), lambda i, j: (layer, 0, j)),
            pl.BlockSpec((None, d, tf), lambda i, j: (layer, 0, j)),
            pl.BlockSpec((None, tf, d), lambda i, j: (layer, j, 0)),
        ],
        out_specs=pl.BlockSpec((tm, d), lambda i, j: (i, 0)),
        out_shape=SDS((m, d), F32),
        scratch_shapes=[pltpu.VMEM((tm, d), BF16)],
        compiler_params=_cparams(("parallel", "arbitrary")),
        name="ffn",
    )(x, g, w1, w3, w2)


def _out_proj_body(x_ref, a_ref, b_ref, wa_ref, wb_ref, o_ref):
    o_ref[...] = x_ref[...] + _dot(a_ref[...], wa_ref[...]) + _dot(b_ref[...], wb_ref[...])


def out_proj(x, a, b, wa, wb, tm):
    m, d = x.shape
    ka, kb = a.shape[1], b.shape[1]
    return pl.pallas_call(
        _out_proj_body,
        grid=(m // tm,),
        in_specs=[
            pl.BlockSpec((tm, d), lambda i: (i, 0)),
            pl.BlockSpec((tm, ka), lambda i: (i, 0)),
            pl.BlockSpec((tm, kb), lambda i: (i, 0)),
            pl.BlockSpec((ka, d), lambda i: (0, 0)),
            pl.BlockSpec((kb, d), lambda i: (0, 0)),
        ],
        out_specs=pl.BlockSpec((tm, d), lambda i: (i, 0)),
        out_shape=SDS((m, d), F32),
        compiler_params=_cparams(("parallel",)),
        name="out_proj",
    )(x, a, b, wa, wb)


S5_SLABS = 8
S5_ROWS = 32


def _s5_body(u_ref, h0_ref, lam_ref, bw_ref, cw_ref, d_ref, wg_ref, bg_ref,
             o_ref, hl_ref, bu_ref, hs_ref, carry_ref, *, tm, n_seq, seq_len):
    first_tile = pl.program_id(1) == 0
    u = u_ref[...]
    ub = u.astype(BF16)
    nr = 2 * S5_ROWS
    for j in range(S5_SLABS):
        res = _dot(ub[:, LANES * j:LANES * (j + 1)], bw_ref[j])
        for q in range(4):
            bu_ref[pl.ds(4 * j + q, tm, stride=nr), :] = res[:, LANES * q:LANES * (q + 1)]
            bu_ref[pl.ds(S5_ROWS + 4 * j + q, tm, stride=nr), :] = res[:, 512 + LANES * q:512 + LANES * (q + 1)]

    lr = lam_ref[0]
    li = lam_ref[1]

    def seq_body(s, _):
        hr = jnp.where(first_tile, h0_ref[s, 0], carry_ref[0])
        hi = jnp.where(first_tile, h0_ref[s, 1], carry_ref[1])

        def step(t, c):
            hr, hi = c
            row = pl.multiple_of((s * seq_len + t) * nr, nr)
            br = bu_ref[pl.ds(row, S5_ROWS), :]
            bi = bu_ref[pl.ds(row + S5_ROWS, S5_ROWS), :]
            nhr = lr * hr - li * hi + br
            nhi = lr * hi + li * hr + bi
            hs_ref[pl.ds(row, S5_ROWS), :] = nhr
            hs_ref[pl.ds(row + S5_ROWS, S5_ROWS), :] = nhi
            return nhr, nhi

        hr, hi = lax.fori_loop(0, seq_len, step, (hr, hi), unroll=min(seq_len, 8))
        carry_ref[0] = hr
        carry_ref[1] = hi
        hl_ref[s, 0] = hr
        hl_ref[s, 1] = hi
        return 0

    lax.fori_loop(0, n_seq, seq_body, 0)

    ys = []
    for j in range(S5_SLABS):
        parts = [hs_ref[pl.ds(4 * j + q, tm, stride=nr), :] for q in range(4)]
        parts += [hs_ref[pl.ds(S5_ROWS + 4 * j + q, tm, stride=nr), :] for q in range(4)]
        lhs = jnp.concatenate(parts, axis=1).astype(BF16)
        ys.append(_dot(lhs, cw_ref[j]))
    y = jnp.concatenate(ys, axis=1) + d_ref[...] * u
    z = jax.nn.gelu(y)
    gate = jax.nn.sigmoid(_dot(z.astype(BF16), wg_ref[...]) + bg_ref[...])
    o_ref[...] = (z * gate).astype(o_ref.dtype)


def s5_mix(proj, h0, lam, bw, cw, d, wg, bg, *, n_groups, tiles_per_group, tm, n_seq, seq_len):
    width = S5_SLABS * LANES
    m = proj.shape[0]
    kern = functools.partial(_s5_body, tm=tm, n_seq=n_seq, seq_len=seq_len)
    const3 = lambda g, t: (0, 0, 0)
    return pl.pallas_call(
        kern,
        grid=(n_groups, tiles_per_group),
        in_specs=[
            pl.BlockSpec((tm, width), lambda g, t: (g * tiles_per_group + t, 0)),
            pl.BlockSpec((n_seq, 2, S5_ROWS, LANES), lambda g, t: (g, 0, 0, 0)),
            pl.BlockSpec((2, S5_ROWS, LANES), const3),
            pl.BlockSpec((S5_SLABS, LANES, 1024), const3),
            pl.BlockSpec((S5_SLABS, 1024, LANES), const3),
            pl.BlockSpec((1, width), lambda g, t: (0, 0)),
            pl.BlockSpec((width, width), lambda g, t: (0, 0)),
            pl.BlockSpec((1, width), lambda g, t: (0, 0)),
        ],
        out_specs=[
            pl.BlockSpec((tm, width), lambda g, t: (g * tiles_per_group + t, 0)),
            pl.BlockSpec((n_seq, 2, S5_ROWS, LANES), lambda g, t: (g, 0, 0, 0)),
        ],
        out_shape=[SDS((m, width), BF16), SDS(h0.shape, F32)],
        scratch_shapes=[
            pltpu.VMEM((tm * 2 * S5_ROWS, LANES), F32),
            pltpu.VMEM((tm * 2 * S5_ROWS, LANES), F32),
            pltpu.VMEM((2, S5_ROWS, LANES), F32),
        ],
        compiler_params=_cparams(("parallel", "arbitrary")),
        name="s5_mix",
    )(proj, h0, lam, bw, cw, d, wg, bg)


def _pair_swap(y, half):
    n = y.shape[-1]
    lane = lax.broadcasted_iota(jnp.int32, y.shape, y.ndim - 1)
    up = pltpu.roll(y, n - half, y.ndim - 1)
    dn = pltpu.roll(y, half, y.ndim - 1)
    return jnp.where((lane % (2 * half)) < half, up, dn)


def _mla_prep_body(qn_ref, qr_ref, cl_ref, kr_ref, cos_ref, sin_ref, gq_ref, gqr_ref, glat_ref, gkr_ref,
                   wukt_ref, wuk_ref, grp_ref, sel_ref, selt_ref,
                   qlat_ref, qrope_ref, klat_ref, klatb_ref, krope_ref, kropeb_ref, krs_ref, krst_ref):
    half = D_ROPE // 2
    cos = cos_ref[...]
    sin = sin_ref[...]
    for h in range(MLA_HEADS):
        x = qn_ref[:, D_NOPE * h:D_NOPE * (h + 1)]
        y = _rms(x, gq_ref[...])
        ql = _dot(y.astype(BF16), wukt_ref[h]) * Q_SCALE
        qlat_ref[:, D_LATENT * h:D_LATENT * (h + 1)] = ql.astype(BF16)
    x = qr_ref[...]
    hi, lo = _split_bf16(x * x)
    ms = (_dot(hi, grp_ref[...]) + _dot(lo, grp_ref[...])) * (1.0 / D_ROPE)
    y = x * lax.rsqrt(ms + EPS) * gqr_ref[...]
    cos4 = jnp.concatenate([cos] * 4, axis=1)
    sin4 = jnp.concatenate([sin] * 4, axis=1)
    qrope_ref[...] = ((y * cos4 + _pair_swap(y, half) * sin4) * Q_SCALE).astype(BF16)
    cl = _rms(cl_ref[...], glat_ref[...])
    klat_ref[...] = cl
    clb = cl.astype(BF16)
    klatb_ref[...] = clb
    kf = _dot(clb, wuk_ref[...])
    hi, lo = _split_bf16(kf * kf)
    ss = _dot(hi, sel_ref[...]) + _dot(lo, sel_ref[...])
    krs_ref[...] = lax.rsqrt(ss[:, 0:MLA_HEADS] * (1.0 / D_NOPE) + EPS)
    sst = _dot_nt(selt_ref[...], hi) + _dot_nt(selt_ref[...], lo)
    krst_ref[...] = lax.rsqrt(sst * (1.0 / D_NOPE) + EPS)
    x = kr_ref[...]
    ms = jnp.sum(x * x, axis=-1, keepdims=True) * (1.0 / LANES)
    y = x * lax.rsqrt(ms + EPS) * gkr_ref[...]
    y = y * cos + _pair_swap(y, half) * sin
    krope_ref[...] = y[:, 0:D_ROPE]
    lane = lax.broadcasted_iota(jnp.int32, y.shape, 1)
    yb = y.astype(BF16)
    zero = jnp.zeros_like(yb)
    kropeb_ref[:, 0:LANES] = jnp.where(lane < D_ROPE, yb, zero)
    kropeb_ref[:, LANES:2 * LANES] = jnp.where(lane >= D_ROPE, yb, zero)


def mla_prep(proj, cos, sin, gq, gqr, glat, gkr, wukt, wuk, grp, sel, selt, tm):
    m = proj.shape[0]
    h = MLA_HEADS
    row = lambda i: (i, 0)
    c2 = lambda i: (0, 0)
    c3 = lambda i: (0, 0, 0)
    return pl.pallas_call(
        _mla_prep_body,
        grid=(m // tm,),
        in_specs=[
            pl.BlockSpec((tm, 1024), lambda i: (i, 1)),
            pl.BlockSpec((tm, 512), lambda i: (i, 4)),
            pl.BlockSpec((tm, 512), lambda i: (i, 5)),
            pl.BlockSpec((tm, LANES), lambda i: (i, 24)),
            pl.BlockSpec((tm, LANES), row),
            pl.BlockSpec((tm, LANES), row),
            pl.BlockSpec((1, D_NOPE), c2),
            pl.BlockSpec((1, 512), c2),
            pl.BlockSpec((1, D_LATENT), c2),
            pl.BlockSpec((1, LANES), c2),
            pl.BlockSpec((h, D_NOPE, D_LATENT), c3),
            pl.BlockSpec((D_LATENT, h * D_NOPE), c2),
            pl.BlockSpec((512, 512), c2),
            pl.BlockSpec((h * D_NOPE, LANES), c2),
            pl.BlockSpec((h, h * D_NOPE), c2),
        ],
        out_specs=[
            pl.BlockSpec((tm, h * D_LATENT), row),
            pl.BlockSpec((tm, 512), row),
            pl.BlockSpec((tm, D_LATENT), row),
            pl.BlockSpec((tm, D_LATENT), row),
            pl.BlockSpec((tm, D_ROPE), row),
            pl.BlockSpec((tm, 2 * LANES), row),
            pl.BlockSpec((tm, h), row),
            pl.BlockSpec((h, tm), lambda i: (0, i)),
        ],
        out_shape=[
            SDS((m, h * D_LATENT), BF16),
            SDS((m, 512), BF16),
            SDS((m, D_LATENT), F32),
            SDS((m, D_LATENT), BF16),
            SDS((m, D_ROPE), F32),
            SDS((m, 2 * LANES), BF16),
            SDS((m, h), F32),
            SDS((h, m), F32),
        ],
        compiler_params=_cparams(("parallel",)),
        name="mla_prep",
    )(proj, proj, proj, proj, cos, sin, gq, gqr, glat, gkr, wukt, wuk, grp, sel, selt)


def _attn_prompt_body(qi_ref, ki_ref, last_ref, ql_ref, qr_ref, kl_ref, kr_ref, rst_ref, wuv_ref,
                      o_ref, m_ref, l_ref, acc_ref, *, tq, tk):
    s_idx = pl.program_id(1)
    qi = qi_ref[s_idx]
    ki = ki_ref[s_idx]

    @pl.when(ki == 0)
    def _():
        m_ref[...] = jnp.full(m_ref.shape, NEG, F32)
        l_ref[...] = jnp.zeros(l_ref.shape, F32)
        acc_ref[...] = jnp.zeros(acc_ref.shape, F32)

    kl = kl_ref[...]
    rows = lax.broadcasted_iota(jnp.int32, (tq, tk), 0)
    cols = lax.broadcasted_iota(jnp.int32, (tq, tk), 1)
    keep = cols + (ki * tk - qi * tq) <= rows
    for h in range(MLA_HEADS):
        j, half = h % 4, h // 4
        s = _dot_nt(ql_ref[:, D_LATENT * h:D_LATENT * (h + 1)], kl) * rst_ref[h:h + 1, :]
        s = s + _dot_nt(qr_ref[:, LANES * j:LANES * (j + 1)], kr_ref[:, LANES * half:LANES * (half + 1)])
        s = jnp.where(keep, s, NEG)
        m_old = m_ref[h]
        m_new = jnp.maximum(m_old, jnp.max(s, axis=-1, keepdims=True))
        p = jnp.exp2(s - m_new)
        alpha = jnp.exp2(m_old - m_new)
        l_ref[h] = alpha * l_ref[h] + jnp.sum(p, axis=-1, keepdims=True)
        acc_ref[h] = alpha * acc_ref[h] + _dot(p.astype(BF16), kl)
        m_ref[h] = m_new

    @pl.when(last_ref[s_idx] == 1)
    def _():
        for h in range(MLA_HEADS):
            o_lat = acc_ref[h] / l_ref[h]
            o_ref[:, D_V * h:D_V * (h + 1)] = _dot(o_lat.astype(BF16), wuv_ref[h]).astype(o_ref.dtype)


def attn_prompt(qlat, qrope, klat, krope, krst, wuv, *, batch, seq, tq, tk):
    nq = seq // tq
    nk = seq // tk
    n_keys = [-(-((q + 1) * tq) // tk) for q in range(nq)]
    qi_tab = np.concatenate([np.full(n, q, np.int32) for q, n in enumerate(n_keys)])
    ki_tab = np.concatenate([np.arange(n, dtype=np.int32) for n in n_keys])
    last_tab = np.concatenate([(np.arange(n) == n - 1).astype(np.int32) for n in n_keys])
    n_tri = len(qi_tab)
    h = MLA_HEADS
    kern = functools.partial(_attn_prompt_body, tq=tq, tk=tk)
    grid_spec = pltpu.PrefetchScalarGridSpec(
        num_scalar_prefetch=3,
        grid=(batch, n_tri),
        in_specs=[
            pl.BlockSpec((tq, h * D_LATENT), lambda b, s, qi, ki, la: (b * nq + qi[s], 0)),
            pl.BlockSpec((tq, 512), lambda b, s, qi, ki, la: (b * nq + qi[s], 0)),
            pl.BlockSpec((tk, D_LATENT), lambda b, s, qi, ki, la: (b * nk + ki[s], 0)),
            pl.BlockSpec((tk, 2 * LANES), lambda b, s, qi, ki, la: (b * nk + ki[s], 0)),
            pl.BlockSpec((h, tk), lambda b, s, qi, ki, la: (0, b * nk + ki[s])),
            pl.BlockSpec((h, D_LATENT, D_V), lambda b, s, qi, ki, la: (0, 0, 0)),
        ],
        out_specs=pl.BlockSpec((tq, h * D_V), lambda b, s, qi, ki, la: (b * nq + qi[s], 0)),
        scratch_shapes=[
            pltpu.VMEM((h, tq, 1), F32),
            pltpu.VMEM((h, tq, 1), F32),
            pltpu.VMEM((h, tq, D_LATENT), F32),
        ],
    )
    return pl.pallas_call(
        kern,
        grid_spec=grid_spec,
        out_shape=SDS((batch * seq, h * D_V), BF16),
        compiler_params=_cparams(("parallel", "arbitrary")),
        name="attn_prompt",
    )(jnp.asarray(qi_tab), jnp.asarray(ki_tab), jnp.asarray(last_tab), qlat, qrope, klat, krope, krst, wuv)


def _attn_paged_body(pt_ref, ql_ref, qr_ref, lat_hbm, rope_hbm, rs_hbm, klo_ref, kro_ref, rso_ref, o_ref,
                     lat_buf, rope_buf, rs_buf, sem, m_ref, l_ref, acc_ref, *, layer, n_pages, grp, dec_seq):
    seq = pl.program_id(0)
    n_seq = pl.num_programs(0)
    n_groups = n_pages // grp
    rows = dec_seq * MLA_HEADS

    def page_copies(pages, slot):
        cps = []
        for i in range(grp):
            cps.append(pltpu.make_async_copy(lat_hbm.at[layer, pages[i]], lat_buf.at[slot, i], sem.at[0, slot]))
            cps.append(pltpu.make_async_copy(rope_hbm.at[layer, pages[i]], rope_buf.at[slot, i], sem.at[1, slot]))
            cps.append(pltpu.make_async_copy(rs_hbm.at[layer, pages[i]], rs_buf.at[slot, i], sem.at[2, slot]))
        return cps

    def start_group(s, g, slot):
        base = s * n_pages + g * grp
        for c in page_copies([pt_ref[base + i] for i in range(grp)], slot):
            c.start()

    def wait_group(slot):
        for c in page_copies([0] * grp, slot):
            c.wait()

    @pl.when(seq == 0)
    def _():
        start_group(0, 0, 0)

    m_ref[...] = jnp.full(m_ref.shape, NEG, F32)
    l_ref[...] = jnp.zeros(l_ref.shape, F32)
    acc_ref[...] = jnp.zeros(acc_ref.shape, F32)
    ql = ql_ref[...]
    qr = qr_ref[...]

    def update(s, kl):
        m_old = m_ref[...]
        m_new = jnp.maximum(m_old, jnp.max(s, axis=-1, keepdims=True))
        p = jnp.exp2(s - m_new)
        alpha = jnp.exp2(m_old - m_new)
        l_ref[...] = alpha * l_ref[...] + jnp.sum(p, axis=-1, keepdims=True)
        acc_ref[...] = alpha * acc_ref[...] + _dot(p.astype(BF16), kl)
        m_ref[...] = m_new

    def scores(kl, krt, rst, n):
        s = _dot_nt(ql, kl)
        s = (s.reshape(dec_seq, MLA_HEADS, n) * rst[None]).reshape(rows, n)
        return s + _dot(qr, krt)

    def group(g, _):
        t = seq * n_groups + g
        slot = t % 2
        wait_group(slot)
        last_of_seq = g == n_groups - 1

        @pl.when(t + 1 < n_seq * n_groups)
        def _():
            start_group(jnp.where(last_of_seq, seq + 1, seq), jnp.where(last_of_seq, 0, g + 1), 1 - slot)

        tk = grp * PAGE
        kl = lat_buf[slot].reshape(tk, D_LATENT).astype(BF16)
        krt = jnp.concatenate([rope_buf[slot, i] for i in range(grp)], axis=1).astype(BF16)
        rst = jnp.concatenate([rs_buf[slot, i] for i in range(grp)], axis=1)
        update(scores(kl, krt, rst, tk), kl)
        return 0

    lax.fori_loop(0, n_groups, group, 0)

    klo = klo_ref[...]
    s = scores(klo, kro_ref[...], rso_ref[...], PAGE)
    tok = lax.broadcasted_iota(jnp.int32, (rows, PAGE), 0) // MLA_HEADS
    key = lax.broadcasted_iota(jnp.int32, (rows, PAGE), 1)
    s = jnp.where(key <= tok, s, NEG)
    update(s, klo)
    o_ref[...] = (acc_ref[...] / l_ref[...]).astype(o_ref.dtype)


def attn_paged(page_table, qlat, qrope, cache_lat, cache_rope_t, cache_rs_t, klat_own, krope_own_t, rst_own,
               *, layer, grp):
    n_seq, n_pages = page_table.shape
    rows = qlat.shape[1]
    dec_seq = rows // MLA_HEADS
    kern = functools.partial(_attn_paged_body, layer=layer, n_pages=n_pages, grp=grp, dec_seq=dec_seq)
    own = lambda s, pt: (s, 0, 0)
    hbm = pl.BlockSpec(memory_space=pl.ANY)
    grid_spec = pltpu.PrefetchScalarGridSpec(
        num_scalar_prefetch=1,
        grid=(n_seq,),
        in_specs=[
            pl.BlockSpec((None, rows, D_LATENT), own),
            pl.BlockSpec((None, rows, D_ROPE), own),
            hbm, hbm, hbm,
            pl.BlockSpec((None, PAGE, D_LATENT), own),
            pl.BlockSpec((None, D_ROPE, PAGE), own),
            pl.BlockSpec((None, MLA_HEADS, PAGE), own),
        ],
        out_specs=pl.BlockSpec((None, rows, D_LATENT), own),
        scratch_shapes=[
            pltpu.VMEM((2, grp, PAGE, D_LATENT), F32),
            pltpu.VMEM((2, grp, D_ROPE, PAGE), F32),
            pltpu.VMEM((2, grp, MLA_HEADS, PAGE), F32),
            pltpu.SemaphoreType.DMA((3, 2)),
            pltpu.VMEM((rows, 1), F32),
            pltpu.VMEM((rows, 1), F32),
            pltpu.VMEM((rows, D_LATENT), F32),
        ],
    )
    return pl.pallas_call(
        kern,
        grid_spec=grid_spec,
        out_shape=SDS((n_seq, rows, D_LATENT), BF16),
        compiler_params=_cparams(("arbitrary",)),
        name="attn_paged",
    )(page_table.reshape(-1), qlat, qrope, cache_lat, cache_rope_t, cache_rs_t, klat_own, krope_own_t, rst_own)


def _head_matmul_body(x_ref, w_ref, o_ref):
    o_ref[...] = _dot(x_ref[...], w_ref[...]).astype(o_ref.dtype)


def head_matmul(x, w):
    m = x.shape[0]
    h, k, n = w.shape
    return pl.pallas_call(
        _head_matmul_body,
        grid=(h,),
        in_specs=[pl.BlockSpec((m, k), lambda i: (0, i)), pl.BlockSpec((None, k, n), lambda i: (i, 0, 0))],
        out_specs=pl.BlockSpec((m, n), lambda i: (0, i)),
        out_shape=SDS((m, h * n), BF16),
        compiler_params=_cparams(("parallel",)),
        name="head_matmul",
    )(x, w)


def _cumsum_rows(x, c):
    r = lax.broadcasted_iota(jnp.int32, (c, c), 0)
    s = lax.broadcasted_iota(jnp.int32, (c, c), 1)
    tri = (s <= r).astype(BF16)
    hi, lo = _split_bf16(x)
    return _dot(tri, hi) + _dot(tri, lo)


def _gla_body(q_ref, k_ref, v_ref, gg_ref, sm_ref, wa_ref, ba_ref, gn_ref, s0_ref,
              o_ref, so_ref, *scratch, c, inner, carried, heads):
    ci = pl.program_id(2)
    nb = c // SUBLANES
    sub = lax.broadcasted_iota(jnp.int32, (SUBLANES, 1), 0)
    st_refs, b_refs, oi_refs = scratch[0:heads], scratch[heads:2 * heads], scratch[2 * heads:3 * heads]

    if carried:
        @pl.when(ci == 0)
        def _():
            for hh in range(heads):
                st_refs[hh][...] = s0_ref[0, hh].T

    def head_unit(u, hh):
        st_ref, b_ref, oi_ref = st_refs[hh], b_refs[hh], oi_refs[hh]
        dk = slice(GLA_DK * hh, GLA_DK * (hh + 1))
        dv = slice(GLA_DV * hh, GLA_DV * (hh + 1))
        r0 = pl.multiple_of(u * c, c)
        rows = pl.ds(r0, c)
        st = st_ref[...] if carried else s0_ref[u, hh].T
        x = _dot(sm_ref[rows, :].astype(BF16), wa_ref[:, dk]) + ba_ref[:, dk]
        g = _log_sigmoid(x) * (1.0 / GLA_GATE_NORM)
        b = _cumsum_rows(g, c)
        q = q_ref[rows, dk]
        k = k_ref[rows, dk]
        v = v_ref[rows, dv]
        o_inter = _dot_nt((q * jnp.exp(b)).astype(BF16), st.astype(BF16))

        in_regs = nb == 1
        if not in_regs:
            b_ref[...] = b
        oi_rows = []
        for bi in range(nb):
            accs = [jnp.zeros((SUBLANES, GLA_DV), F32)] * SUBLANES
            for bj in range(bi + 1):
                kj = k[SUBLANES * bj:SUBLANES * (bj + 1), :]
                vj = v[SUBLANES * bj:SUBLANES * (bj + 1), :]
                bj_rows = b if in_regs else b_ref[SUBLANES * bj:SUBLANES * (bj + 1), :]
                for tt in range(SUBLANES):
                    t = SUBLANES * bi + tt
                    qt = q[t:t + 1, :]
                    bt = b[t:t + 1, :] if in_regs else b_ref[t:t + 1, :]
                    dlt = bt - bj_rows
                    if bj == bi:
                        dlt = jnp.where(sub <= tt, dlt, NEG)
                    a = jnp.sum(jnp.exp(dlt) * kj * qt, axis=-1, keepdims=True)
                    accs[tt] = accs[tt] + a * vj
            for tt in range(SUBLANES):
                t = SUBLANES * bi + tt
                row_t = jnp.sum(accs[tt], axis=0, keepdims=True)
                if in_regs:
                    oi_rows.append(row_t)
                else:
                    oi_ref[t:t + 1, :] = row_t
        o_intra = jnp.concatenate(oi_rows, axis=0) if in_regs else oi_ref[...]

        o = (o_inter + o_intra) * (GLA_DK ** -0.5)
        o = _rms(o, gn_ref[...])
        gg = gg_ref[rows, dv]
        o_ref[rows, dv] = (o * (gg * jax.nn.sigmoid(gg))).astype(o_ref.dtype)

        b_last = b[c - 1:c, :]
        kd = k * jnp.exp(b_last - b)
        st_new = st * jnp.exp(b_last) + _dot_tn(v.astype(BF16), kd.astype(BF16))
        if carried:
            st_ref[...] = st_new
        else:
            so_ref[u, hh] = st_new.T

    def unit(u, _):
        for hh in range(heads):
            head_unit(u, hh)
        return 0

    lax.fori_loop(0, inner, unit, 0, unroll=1 if carried else 2)

    if carried:
        @pl.when(ci == pl.num_programs(2) - 1)
        def _():
            for hh in range(heads):
                so_ref[0, hh] = st_refs[hh][...].T


def gla_mix(proj, wa, ba, gn, s0, *, n_seq, n_chunks, c):
    m = proj.shape[0]
    hh = GLA_HEADS
    carried = n_chunks > 1
    if carried:
        inner = _pick(n_chunks, (4, 2, 1))
        seqs, steps = 1, n_chunks // inner
    else:
        inner = _pick(n_seq, (16, 8, 4, 2, 1))
        seqs, steps = inner, 1
    rows = inner * c
    hg = 2
    kern = functools.partial(_gla_body, c=c, inner=inner, carried=carried, heads=hg)
    row = lambda g, h, ci: g * steps + ci
    st4 = lambda g, h, ci: (g, h, 0, 0)
    dk, dv = hg * GLA_DK, hg * GLA_DV
    return pl.pallas_call(
        kern,
        grid=(n_seq // seqs, hh // hg, steps),
        in_specs=[
            pl.BlockSpec((rows, dk), lambda g, h, ci: (row(g, h, ci), h)),
            pl.BlockSpec((rows, dk), lambda g, h, ci: (row(g, h, ci), hh // hg + h)),
            pl.BlockSpec((rows, dv), lambda g, h, ci: (row(g, h, ci), hh // hg + h)),
            pl.BlockSpec((rows, dv), lambda g, h, ci: (row(g, h, ci), 2 * (hh // hg) + h)),
            pl.BlockSpec((rows, LANES), lambda g, h, ci: (row(g, h, ci), 48)),
            pl.BlockSpec((LANES, dk), lambda g, h, ci: (0, h)),
            pl.BlockSpec((1, dk), lambda g, h, ci: (0, h)),
            pl.BlockSpec((1, GLA_DV), lambda g, h, ci: (0, 0)),
            pl.BlockSpec((seqs, hg, GLA_DK, GLA_DV), st4),
        ],
        out_specs=[
            pl.BlockSpec((rows, dv), lambda g, h, ci: (row(g, h, ci), h)),
            pl.BlockSpec((seqs, hg, GLA_DK, GLA_DV), st4),
        ],
        out_shape=[SDS((m, hh * GLA_DV), BF16), SDS((n_seq, hh, GLA_DK, GLA_DV), F32)],
        scratch_shapes=(
            [pltpu.VMEM((GLA_DV, GLA_DK), F32)] * hg + [pltpu.VMEM((c, GLA_DK), F32)] * hg
            + [pltpu.VMEM((c, GLA_DV), F32)] * hg
        ),
        compiler_params=_cparams(("parallel", "parallel", "arbitrary")),
        name="gla_mix",
    )(proj, proj, proj, proj, proj, wa, ba, gn, s0)


ML_I_LANE = GLA_RANK
ML_F_LANE = GLA_RANK + ML_HEADS


def _mlstm_body(q_ref, k_ref, v_ref, mo_ref, sm_ref, bias_ref, gn_ref, c0_ref, n0_ref, m0_ref,
                o_ref, co_ref, no_ref, mo_out_ref, cs_ref, ns_ref, ms_ref, tr_ref, *, c, inner, carried):
    ci = pl.program_id(1)
    r = lax.broadcasted_iota(jnp.int32, (c, c), 0)
    s = lax.broadcasted_iota(jnp.int32, (c, c), 1)
    causal = s <= r
    lane_h = lax.broadcasted_iota(jnp.int32, (1, ML_HEADS), 1)

    if carried:
        @pl.when(ci == 0)
        def _():
            cs_ref[...] = c0_ref[0]
            ns_ref[...] = n0_ref[0]
            ms_ref[...] = m0_ref[0]

    def unit(u, _):
        r0 = pl.multiple_of(u * c, c)
        rows = pl.ds(r0, c)
        pre = sm_ref[rows, :] + bias_ref[...]
        bcum = _cumsum_rows(_log_sigmoid(pre), c)
        tr_ref[0] = pre.T
        tr_ref[1] = bcum.T
        m_all = ms_ref[...] if carried else m0_ref[u]
        m_row = jnp.zeros((1, ML_HEADS), F32)
        for h in range(ML_HEADS):
            li, lf = ML_I_LANE + h, ML_F_LANE + h
            i_col = pre[:, li:li + 1]
            b_col = bcum[:, lf:lf + 1]
            i_row = tr_ref[0, li:li + 1, :]
            b_row = tr_ref[1, lf:lf + 1, :]
            m_prev = m_all[:, h:h + 1]
            rel = jnp.where(causal, b_col - b_row + i_row, NEG)
            carry_log = b_col + m_prev
            m_t = jnp.maximum(carry_log, jnp.max(rel, axis=-1, keepdims=True))
            w_intra = jnp.exp(rel - m_t)
            w_carry = jnp.exp(carry_log - m_t)

            q = q_ref[rows, ML_DK * h:ML_DK * (h + 1)] * (ML_DK ** -0.5)
            k = k_ref[rows, ML_DK * h:ML_DK * (h + 1)]
            v = v_ref[rows, ML_DV * h:ML_DV * (h + 1)]
            qb = q.astype(BF16)
            vb = v.astype(BF16)
            cm = cs_ref[h] if carried else c0_ref[u, h]
            nv = ns_ref[h:h + 1, :] if carried else n0_ref[u, h:h + 1, :]
            qk = _dot_nt(qb, k.astype(BF16)) * w_intra
            num = _dot(qk.astype(BF16), vb) + w_carry * _dot(qb, cm.astype(BF16))
            den = jnp.sum(qk, axis=-1, keepdims=True) + w_carry * jnp.sum(q * nv, axis=-1, keepdims=True)
            hh = num / jnp.maximum(jnp.abs(den), jnp.exp(-m_t))

            mo = mo_ref[rows, ML_DV * h:ML_DV * (h + 1)]
            o_ref[rows, ML_DV * h:ML_DV * (h + 1)] = (jax.nn.sigmoid(mo) * _rms(hh, gn_ref[...])).astype(o_ref.dtype)

            m_new = m_t[c - 1:c, :]
            b_last = b_col[c - 1:c, :]
            w_state = jnp.exp(b_last - b_col + i_col - m_new)
            decay = jnp.exp(b_last + m_prev - m_new)
            kw = k * w_state
            cm_new = decay * cm + _dot_tn(kw.astype(BF16), vb)
            nv_new = decay * nv + jnp.sum(kw, axis=0, keepdims=True)
            if carried:
                cs_ref[h] = cm_new
                ns_ref[h:h + 1, :] = nv_new
            else:
                co_ref[u, h] = cm_new
                no_ref[u, h:h + 1, :] = nv_new
            m_row = jnp.where(lane_h == h, m_new, m_row)
        if carried:
            ms_ref[...] = m_row
        else:
            mo_out_ref[u] = m_row
        return 0

    lax.fori_loop(0, inner, unit, 0)

    if carried:
        @pl.when(ci == pl.num_programs(1) - 1)
        def _():
            co_ref[0] = cs_ref[...]
            no_ref[0] = ns_ref[...]
            mo_out_ref[0] = ms_ref[...]


def mlstm_mix(proj, bias, gn, c0, n0, m0, *, n_seq, n_chunks, c):
    m = proj.shape[0]
    hh = ML_HEADS
    carried = n_chunks > 1
    if carried:
        inner = _pick(n_chunks, (4, 2, 1))
        seqs, steps = 1, n_chunks // inner
    else:
        inner = _pick(n_seq, (8, 4, 2, 1))
        seqs, steps = inner, 1
    rows = inner * c
    kern = functools.partial(_mlstm_body, c=c, inner=inner, carried=carried)
    row = lambda g, ci: g * steps + ci
    st4 = lambda g, ci: (g, 0, 0, 0)
    st3 = lambda g, ci: (g, 0, 0)
    return pl.pallas_call(
        kern,
        grid=(n_seq // seqs, steps),
        in_specs=[
            pl.BlockSpec((rows, hh * ML_DK), lambda g, ci: (row(g, ci), 6)),
            pl.BlockSpec((rows, hh * ML_DK), lambda g, ci: (row(g, ci), 7)),
            pl.BlockSpec((rows, hh * ML_DV), lambda g, ci: (row(g, ci), 4)),
            pl.BlockSpec((rows, hh * ML_DV), lambda g, ci: (row(g, ci), 5)),
            pl.BlockSpec((rows, LANES), lambda g, ci: (row(g, ci), 48)),
            pl.BlockSpec((1, LANES), lambda g, ci: (0, 0)),
            pl.BlockSpec((1, ML_DV), lambda g, ci: (0, 0)),
            pl.BlockSpec((seqs, hh, ML_DK, ML_DV), st4),
            pl.BlockSpec((seqs, hh, ML_DK), st3),
            pl.BlockSpec((seqs, 1, hh), st3),
        ],
        out_specs=[
            pl.BlockSpec((rows, hh * ML_DV), lambda g, ci: (row(g, ci), 0)),
            pl.BlockSpec((seqs, hh, ML_DK, ML_DV), st4),
            pl.BlockSpec((seqs, hh, ML_DK), st3),
            pl.BlockSpec((seqs, 1, hh), st3),
        ],
        out_shape=[
            SDS((m, hh * ML_DV), BF16),
            SDS((n_seq, hh, ML_DK, ML_DV), F32),
            SDS((n_seq, hh, ML_DK), F32),
            SDS((n_seq, 1, hh), F32),
        ],
        scratch_shapes=[
            pltpu.VMEM((hh, ML_DK, ML_DV), F32),
            pltpu.VMEM((hh, ML_DK), F32),
            pltpu.VMEM((1, hh), F32),
            pltpu.VMEM((2, LANES, c), F32),
        ],
        compiler_params=_cparams(("parallel", "arbitrary")),
        name="mlstm_mix",
    )(proj, proj, proj, proj, proj, bias, gn, c0, n0, m0)


def _pad_cols(w, n):
    return jnp.pad(w, ((0, 0), (0, n - w.shape[1])))


def _even_in_weight(w):
    d = w.shape[0]
    hq = MLA_HEADS * (D_NOPE + D_ROPE)
    u = w[:, :1024]
    q = w[:, 1024:1024 + hq].reshape(d, MLA_HEADS, D_NOPE + D_ROPE)
    q_nope = q[:, :, :D_NOPE].reshape(d, MLA_HEADS * D_NOPE)
    q_rope = q[:, :, D_NOPE:].reshape(d, 2, 4, D_ROPE).transpose(0, 2, 1, 3).reshape(d, MLA_HEADS * D_ROPE)
    c_lat = w[:, 1024 + hq:1024 + hq + D_LATENT]
    k_r = w[:, 1024 + hq + D_LATENT:]
    return _pad_cols(jnp.concatenate([u, q_nope, q_rope, c_lat, k_r, k_r], axis=1), 3328).astype(BF16)


def _odd_in_weight(w):
    sizes = (512, 512, 1024, GLA_RANK, 1024, 512, 512, 1024, ML_HEADS, ML_HEADS, 1024)
    offs = np.concatenate([[0], np.cumsum(sizes)])
    seg = [w[:, offs[i]:offs[i + 1]] for i in range(len(sizes))]
    gq, gk, gv, ga, gg, mq, mk, mv, mi, mf, mo = seg
    small = _pad_cols(jnp.concatenate([ga, mi, mf], axis=1), LANES)
    return _pad_cols(jnp.concatenate([gq, gk, gv, gg, mq, mk, mv, mo, small], axis=1), 6400).astype(BF16)


def _s5_params(a_re, a_im, log_dt, b_re, b_im, c_re, c_im):
    g, p = a_re.shape
    ar = jnp.minimum(a_re, -1e-4)
    ai = a_im
    dt = jnp.exp(log_dt)[:, None]
    mag = jnp.exp(ar * dt)
    lr = mag * jnp.cos(ai * dt)
    li = mag * jnp.sin(ai * dt)
    den = ar * ar + ai * ai
    cr = ((lr - 1.0) * ar + li * ai) / den
    ci = (li * ar - (lr - 1.0) * ai) / den
    bb_re = cr[..., None] * b_re - ci[..., None] * b_im
    bb_im = cr[..., None] * b_im + ci[..., None] * b_re
    eye = jnp.eye(8, dtype=F32)

    def in_slab(bb):
        x = bb.transpose(0, 2, 1).reshape(S5_SLABS, 8, S5_GROUP, p)
        return jnp.einsum("jaip,ab->jaibp", x, eye).reshape(S5_SLABS, 8 * S5_GROUP, 8 * p)

    def out_slab(cc):
        x = cc.transpose(0, 2, 1).reshape(S5_SLABS, 8, p, S5_GROUP)
        return jnp.einsum("japi,ab->japbi", x, eye).reshape(S5_SLABS, 8 * p, 8 * S5_GROUP)

    bw = jnp.concatenate([in_slab(bb_re), in_slab(bb_im)], axis=2).astype(BF16)
    cw = jnp.concatenate([out_slab(c_re), -out_slab(c_im)], axis=1).astype(BF16)
    lam = jnp.stack([lr.reshape(S5_ROWS, LANES), li.reshape(S5_ROWS, LANES)])
    return lam, bw, cw


def _rope_tables(pos):
    half = D_ROPE // 2
    inv_freq = ROPE_THETA ** (-jnp.arange(half, dtype=F32) / half)
    ang = pos.astype(F32)[:, None] * inv_freq
    cos, sin = jnp.cos(ang), jnp.sin(ang)
    return jnp.tile(cos, (1, 4)), jnp.tile(jnp.concatenate([-sin, sin], axis=1), (1, 2))


def _pick(n, prefs):
    for t in prefs:
        if n % t == 0:
            return t
    return n


def _even_layer(x, *, n_seq, seq_len, pos, h0_re, h0_im, attend, p):
    m = x.shape[0]
    tm = _pick(m, (512, 256))
    proj = norm_matmul(x, p["norm_mix"], p["w_in"], tm, 1664)
    if seq_len >= 256:
        tile = _pick(seq_len, (256,))
        s5_kw = dict(n_groups=n_seq, tiles_per_group=seq_len // tile, tm=tile, n_seq=1, seq_len=tile)
    else:
        per = max(1, 256 // seq_len)
        per = _pick(n_seq, (per, 16, 8, 4, 2, 1))
        s5_kw = dict(n_groups=n_seq // per, tiles_per_group=1, tm=per * seq_len, n_seq=per, seq_len=seq_len)
    h0 = jnp.stack([h0_re.reshape(n_seq, S5_ROWS, LANES), h0_im.reshape(n_seq, S5_ROWS, LANES)], axis=1)
    s5_y, h_last = s5_mix(proj, h0, p["lam"], p["bw"], p["cw"], p["s5_d"], p["w_glu"], p["b_glu"], **s5_kw)
    s5_re = h_last[:, 0].reshape(n_seq, S5_ROWS * LANES // S5_STATE, S5_STATE)
    s5_im = h_last[:, 1].reshape(n_seq, S5_ROWS * LANES // S5_STATE, S5_STATE)
    cos, sin = _rope_tables(pos)
    qlat, qrope, klat, klat_b, krope, krope_b, krs, krst = mla_prep(
        proj, cos, sin, p["g_q"], p["g_qr"], p["g_lat"], p["g_kr"], p["wukt"], p["wuk"], p["grp"], p["sel"],
        p["selt"], Q_TILE)
    o = attend(qlat, qrope, klat_b, krope_b, krst)
    x = out_proj(x, s5_y, o, p["w_out_a"], p["w_out_b"], tm)
    return x, (klat, krope, krs, s5_re, s5_im)


def _odd_layer(x, *, n_seq, seq_len, s0, c0, n0, m0, p):
    m = x.shape[0]
    tm = _pick(m, (512, 256))
    proj = norm_matmul(x, p["norm_mix"], p["w_in"], tm, 1280)
    c = math.gcd(seq_len, CHUNK)
    n_chunks = seq_len // c
    o_gla, s_gla = gla_mix(proj, p["wa"], p["ba"], p["g_gla"], s0, n_seq=n_seq, n_chunks=n_chunks, c=c)
    h_ml, cm, nv, mm = mlstm_mix(proj, p["ml_bias"], p["g_ml"], c0, n0, m0[:, None, :],
                                 n_seq=n_seq, n_chunks=n_chunks, c=c)
    x = out_proj(x, o_gla, h_ml, p["w_out_a"], p["w_out_b"], tm)
    return x, (s_gla, cm, nv, mm[:, 0, :])


def kernel(x_prompt, x_sample, cache_mla_latent, cache_mla_k_rope, cache_mla_k_rscale, state_s5_re, state_s5_im, state_gla, state_mlstm_c, state_mlstm_n, state_mlstm_m, page_table, norm_mix, norm_ffn, ffn_w1, ffn_w3, ffn_w2, e_w_in, s5_a_re, s5_a_im, s5_log_dt, s5_b_re, s5_b_im, s5_c_re, s5_c_im, s5_d, s5_w_glu, s5_b_glu, mla_g_qnope, mla_g_qrope, mla_g_knope, mla_g_krope, mla_g_latent, mla_w_uk, mla_w_uv, e_w_out, o_w_in, gla_w_a2, gla_b_a, gla_g_norm, ml_b_i, ml_b_f, ml_g_norm, o_w_out):
    bp, lp, d = x_prompt.shape
    bs, ls, _ = x_sample.shape
    depth = norm_mix.shape[0]
    past_len = page_table.shape[1] * PAGE
    pos_p = jnp.arange(lp, dtype=jnp.int32)
    pos_s = past_len + jnp.arange(ls, dtype=jnp.int32)
    xp = x_prompt.reshape(bp * lp, d)
    xs = x_sample.reshape(bs * ls, d)
    even_p, even_s, odd_p, odd_s = [], [], [], []
    w1b, w3b, w2b = ffn_w1.astype(BF16), ffn_w3.astype(BF16), ffn_w2.astype(BF16)
    for l in range(depth):
        j = l // 2
        if l % 2 == 0:
            lam, bw, cw = _s5_params(s5_a_re[j], s5_a_im[j], s5_log_dt[j], s5_b_re[j], s5_b_im[j],
                                     s5_c_re[j], s5_c_im[j])
            s5w = s5_a_re.shape[1] * S5_GROUP
            h = MLA_HEADS
            wuk = mla_w_uk[j]
            sel = (jnp.arange(h * D_NOPE)[:, None] // D_NOPE == jnp.arange(LANES)[None, :]).astype(BF16)
            gi = jnp.arange(512) // D_ROPE
            p = dict(
                norm_mix=norm_mix[l][None], w_in=_even_in_weight(e_w_in[j]),
                lam=lam, bw=bw, cw=cw, s5_d=s5_d[j][None], w_glu=s5_w_glu[j].astype(BF16), b_glu=s5_b_glu[j][None],
                g_q=(mla_g_qnope[j] * mla_g_knope[j])[None], g_qr=jnp.tile(mla_g_qrope[j], h)[None],
                g_lat=mla_g_latent[j][None], g_kr=jnp.tile(mla_g_krope[j], 2)[None],
                wukt=wuk.transpose(1, 2, 0).astype(BF16), wuk=wuk.reshape(D_LATENT, h * D_NOPE).astype(BF16),
                grp=(gi[:, None] == gi[None, :]).astype(BF16), sel=sel, selt=sel[:, :h].T,
                w_out_a=e_w_out[j][:s5w].astype(BF16), w_out_b=e_w_out[j][s5w:].astype(BF16),
            )
            wuv = mla_w_uv[j].transpose(1, 0, 2).astype(BF16)

            def attend_p(qlat, qrope, klat_b, krope_b, krst):
                return attn_prompt(qlat, qrope, klat_b, krope_b, krst, wuv, batch=bp, seq=lp,
                                   tq=Q_TILE, tk=_pick(lp, (1024, 512, 256)))

            def attend_s(qlat, qrope, klat_b, krope_b, krst):
                rows = ls * h
                qr = qrope.reshape(bs * ls, 4, 2, D_ROPE).transpose(0, 2, 1, 3).reshape(bs, rows, D_ROPE)
                klat_own = jnp.pad(klat_b.reshape(bs, ls, D_LATENT), ((0, 0), (0, PAGE - ls), (0, 0)))
                lane_pad = ((0, 0), (0, 0), (0, PAGE - ls))
                krope_own_t = jnp.pad(krope_b[:, :D_ROPE].reshape(bs, ls, D_ROPE).transpose(0, 2, 1), lane_pad)
                rst_own = jnp.pad(krst.reshape(h, bs, ls).transpose(1, 0, 2), lane_pad)
                n_pages = page_table.shape[1]
                o_lat = attn_paged(page_table, qlat.reshape(bs, rows, D_LATENT), qr, cache_mla_latent,
                                   jnp.swapaxes(cache_mla_k_rope, 2, 3), jnp.swapaxes(cache_mla_k_rscale, 2, 3),
                                   klat_own, krope_own_t, rst_own,
                                   layer=j, grp=_pick(n_pages, (32, 16, 8, 4, 2, 1)))
                return head_matmul(o_lat.reshape(bs * ls, h * D_LATENT), wuv)

            zeros = jnp.zeros((bp, s5_a_re.shape[1], S5_STATE), F32)
            xp, st_p = _even_layer(xp, n_seq=bp, seq_len=lp, pos=jnp.tile(pos_p, bp), h0_re=zeros, h0_im=zeros,
                                   attend=attend_p, p=p)
            xs, st_s = _even_layer(xs, n_seq=bs, seq_len=ls, pos=jnp.tile(pos_s, bs), h0_re=state_s5_re[j],
                                   h0_im=state_s5_im[j], attend=attend_s, p=p)
            even_p.append((st_p[0].reshape(bp, lp, -1), st_p[1].reshape(bp, lp, -1), st_p[2].reshape(bp, lp, -1),
                           st_p[3], st_p[4]))
            even_s.append((st_s[0].reshape(bs, ls, -1), st_s[1].reshape(bs, ls, -1), st_s[2].reshape(bs, ls, -1),
                           st_s[3], st_s[4]))
        else:
            hk = GLA_HEADS * GLA_DK
            wo = o_w_out[j]
            bias = jnp.zeros((LANES,), F32)
            bias = bias.at[ML_I_LANE:ML_I_LANE + ML_HEADS].set(ml_b_i[j]).at[ML_F_LANE:ML_F_LANE + ML_HEADS].set(ml_b_f[j])
            p = dict(
                norm_mix=norm_mix[l][None], w_in=_odd_in_weight(o_w_in[j]),
                wa=jnp.pad(gla_w_a2[j], ((0, LANES - GLA_RANK), (0, 0))).astype(BF16), ba=gla_b_a[j][None],
                g_gla=gla_g_norm[j][None], ml_bias=bias[None], g_ml=ml_g_norm[j][None],
                w_out_a=wo[:GLA_HEADS * GLA_DV].astype(BF16), w_out_b=wo[GLA_HEADS * GLA_DV:].astype(BF16),
            )
            del hk
            xp, st_p = _odd_layer(xp, n_seq=bp, seq_len=lp, s0=jnp.zeros((bp, GLA_HEADS, GLA_DK, GLA_DV), F32),
                                  c0=jnp.zeros((bp, ML_HEADS, ML_DK, ML_DV), F32),
                                  n0=jnp.zeros((bp, ML_HEADS, ML_DK), F32), m0=jnp.zeros((bp, ML_HEADS), F32), p=p)
            xs, st_s = _odd_layer(xs, n_seq=bs, seq_len=ls, s0=state_gla[j], c0=state_mlstm_c[j],
                                  n0=state_mlstm_n[j], m0=state_mlstm_m[j], p=p)
            odd_p.append(st_p)
            odd_s.append(st_s)
        g = norm_ffn[l][None]
        tf = _pick(w1b.shape[2], (512, 256, 128))
        xp = ffn(xp, g, w1b, w3b, w2b, l, _pick(xp.shape[0], (1024, 512, 256)), tf)
        xs = ffn(xs, g, w1b, w3b, w2b, l, _pick(xs.shape[0], (1024, 512, 256)), tf)
    p_lat, p_rope, p_rscale, p_s5_re, p_s5_im = [jnp.stack(t) for t in zip(*even_p)]
    s_lat, s_rope, s_rscale, s_s5_re, s_s5_im = [jnp.stack(t) for t in zip(*even_s)]
    p_gla, p_ml_c, p_ml_n, p_ml_m = [jnp.stack(t) for t in zip(*odd_p)]
    s_gla, s_ml_c, s_ml_n, s_ml_m = [jnp.stack(t) for t in zip(*odd_s)]
    return (xp.reshape(bp, lp, d), xs.reshape(bs, ls, d),
            p_lat, p_rope, p_rscale, p_s5_re, p_s5_im, p_gla, p_ml_c, p_ml_n, p_ml_m,
            s_lat, s_rope, s_rscale, s_s5_re, s_s5_im, s_gla, s_ml_c, s_ml_n, s_ml_m)
```
